```python
import math
import jax, jax.numpy as jnp
from jax import lax
import numpy as np

D_MODEL = 4096
BATCH = 4
SEQ = 2048
DEPTH = 4
DEC_BATCH = 128
DEC_SEQ = 1
PAST_LEN = 16384
PAGE_SIZE = 128

CONV_A_DIM = D_MODEL // 2
CONV_A_WIDTH = 3
SSM_D_INNER = D_MODEL
SSM_HEAD_DIM = 64
SSM_HEADS = SSM_D_INNER // SSM_HEAD_DIM
SSM_GROUPS = 8
SSM_STATE = 128
SSM_CONV_WIDTH = 4
SSM_CONV_DIM = SSM_D_INNER + 2 * SSM_GROUPS * SSM_STATE
SSD_CHUNK = 128
PEER_HEADS = 8
PEER_N_KEYS = 128
PEER_EXPERTS = PEER_N_KEYS * PEER_N_KEYS
PEER_TOPK = 16
PEER_D_KEY = 256
PEER_D_HALF = PEER_D_KEY // 2
PEER_BLOCK = 64
N_MOD = 6
IN_COLS = 2 * D_MODEL + 3 * CONV_A_DIM + SSM_D_INNER + SSM_CONV_DIM + SSM_HEADS
NORM_EPS = 1e-6

kernel_name = 'hybrid_conv_ssd_peer_adaln_step'


def _rmsnorm(x, g):
    xf = x.astype(jnp.float32)
    xf = xf * lax.rsqrt(jnp.mean(xf * xf, axis=-1, keepdims=True) + NORM_EPS)
    return (xf * g.astype(jnp.float32)).astype(x.dtype)


def _causal_dwconv(x, buf, w, b=None):
    width = w.shape[0]
    xp = jnp.concatenate([buf.astype(x.dtype), x], axis=1)
    y = lax.conv_general_dilated(xp, w[:, None, :].astype(x.dtype), window_strides=(1,), padding='VALID',
                                 dimension_numbers=('NWC', 'WIO', 'NWC'), feature_group_count=x.shape[-1])
    if b is not None:
        y = y + b.astype(x.dtype)
    return y, xp[:, xp.shape[1] - (width - 1):]


def _ssd(x, dt, a, bm, cm, h0):
    bsz, seq_len = x.shape[:2]
    q = min(SSD_CHUNK, seq_len)
    pad = (-seq_len) % q
    padw = lambda t: jnp.pad(t.astype(jnp.float32), [(0, 0), (0, pad)] + [(0, 0)] * (t.ndim - 2))
    nc = (seq_len + pad) // q
    r = SSM_HEADS // SSM_GROUPS
    xc = padw(x).reshape(bsz, nc, q, SSM_GROUPS, r, SSM_HEAD_DIM)
    dtc = padw(dt).reshape(bsz, nc, q, SSM_GROUPS, r)
    bc = padw(bm).reshape(bsz, nc, q, SSM_GROUPS, SSM_STATE)
    cc = padw(cm).reshape(bsz, nc, q, SSM_GROUPS, SSM_STATE)
    acs = jnp.cumsum(dtc * a.reshape(SSM_GROUPS, r), axis=2)
    xdt = xc * dtc[..., None]
    causal = jnp.tril(jnp.ones((q, q), dtype=bool))[:, :, None, None]
    seg = acs[:, :, :, None] - acs[:, :, None, :]
    decay = jnp.exp(jnp.where(causal, seg, -jnp.inf))
    cb = jnp.einsum('bclgn,bcsgn->bclsg', cc, bc)
    y_diag = jnp.einsum('bclsgr,bcsgrp->bclgrp', cb[..., None] * decay, xdt)
    decay_to_end = jnp.exp(acs[:, :, -1:] - acs)
    chunk_states = jnp.einsum('bclgn,bclgr,bclgrp->bcgrpn', bc, decay_to_end, xdt)
    chunk_decay = jnp.exp(acs[:, :, -1])

    def step(h, inp):
        dec, st = inp
        return h * dec[..., None, None] + st, h

    h_init = h0.astype(jnp.float32).reshape(bsz, SSM_GROUPS, r, SSM_HEAD_DIM, SSM_STATE)
    h_last, h_prev = lax.scan(step, h_init, (jnp.moveaxis(chunk_decay, 1, 0), jnp.moveaxis(chunk_states, 1, 0)))
    h_prev = jnp.moveaxis(h_prev, 0, 1)
    y_off = jnp.einsum('bclgn,bcgrpn,bclgr->bclgrp', cc, h_prev, jnp.exp(acs))
    y = (y_diag + y_off).reshape(bsz, nc * q, SSM_HEADS, SSM_HEAD_DIM)[:, :seq_len]
    return y, h_last.reshape(bsz, SSM_HEADS, SSM_HEAD_DIM, SSM_STATE)


def _gated_group_rmsnorm(y, z, g):
    yz = (y * jax.nn.silu(z)).astype(jnp.float32)
    shp = yz.shape
    yg = yz.reshape(shp[:-1] + (SSM_GROUPS, shp[-1] // SSM_GROUPS))
    yg = yg * lax.rsqrt(jnp.mean(yg * yg, axis=-1, keepdims=True) + NORM_EPS)
    return (yg.reshape(shp) * g.astype(jnp.float32)).astype(y.dtype)


def _peer(h, wq, keys, u_tab, v_tab):
    shp = h.shape
    t = h.reshape(-1, D_MODEL)
    n_tok = t.shape[0]
    pad = (-n_tok) % PEER_BLOCK
    blocks = jnp.pad(t, ((0, pad), (0, 0))).reshape(-1, PEER_BLOCK, D_MODEL)

    def one_block(xb):
        qv = (xb @ wq).reshape(PEER_BLOCK, PEER_HEADS, 2, PEER_D_HALF)
        s = jnp.einsum('thsk,hsnk->thsn', qv, keys).astype(jnp.float32)
        s_top, i_top = lax.top_k(s, PEER_TOPK)
        cand = (s_top[:, :, 0, :, None] + s_top[:, :, 1, None, :]).reshape(PEER_BLOCK, PEER_HEADS, PEER_TOPK * PEER_TOPK)
        sc, ci = lax.top_k(cand, PEER_TOPK)
        e1 = jnp.take_along_axis(i_top[:, :, 0], ci // PEER_TOPK, axis=-1)
        e2 = jnp.take_along_axis(i_top[:, :, 1], ci % PEER_TOPK, axis=-1)
        idx = e1 * PEER_N_KEYS + e2
        g = jax.nn.softmax(sc, axis=-1).astype(xb.dtype)
        act = jax.nn.gelu(jnp.einsum('thkd,td->thk', u_tab[idx], xb), approximate=False)
        return jnp.einsum('thk,thkd->td', g * act, v_tab[idx])

    out = lax.map(one_block, blocks).reshape(-1, D_MODEL)[:n_tok]
    return out.reshape(shp)


def _layer(x, c, buf_a, buf_m, ssm_h, w_ada, b_ada, norm1_g, w_in, conv_a_w, w_out_a, conv_m_w, conv_m_b,
           dt_bias, a_log, d_skip, ssm_norm_g, w_out_m, w_o, norm2_g, peer_wq, peer_keys, peer_u, peer_v):
    bsz, seq_len = x.shape[:2]
    mod = (jax.nn.silu(c) @ w_ada + b_ada).reshape(c.shape[0], N_MOD, 1, D_MODEL)
    shift1, scale1, gate1, shift2, scale2, gate2 = [mod[:, i] for i in range(N_MOD)]
    h = _rmsnorm(x, norm1_g) * (1 + scale1) + shift1
    cuts = np.cumsum([D_MODEL, D_MODEL, CONV_A_DIM, CONV_A_DIM, CONV_A_DIM, SSM_D_INNER, SSM_CONV_DIM]).tolist()
    gate_a, gate_m, a_b, a_c, a_h, z, xbc, dt = jnp.split(h @ w_in, cuts, axis=-1)
    conv_a, new_buf_a = _causal_dwconv(a_c * a_h, buf_a, conv_a_w)
    out_a = (a_b * conv_a) @ w_out_a
    xbc, new_buf_m = _causal_dwconv(xbc, buf_m, conv_m_w, conv_m_b)
    xbc = jax.nn.silu(xbc)
    xs, bm, cm = jnp.split(xbc, [SSM_D_INNER, SSM_D_INNER + SSM_GROUPS * SSM_STATE], axis=-1)
    dt = jax.nn.softplus(dt.astype(jnp.float32) + dt_bias.astype(jnp.float32))
    a = -jnp.exp(a_log.astype(jnp.float32))
    xs_h = xs.reshape(bsz, seq_len, SSM_HEADS, SSM_HEAD_DIM)
    y, new_h = _ssd(xs_h, dt, a, bm.reshape(bsz, seq_len, SSM_GROUPS, SSM_STATE),
                    cm.reshape(bsz, seq_len, SSM_GROUPS, SSM_STATE), ssm_h)
    y = (y + d_skip.astype(jnp.float32)[:, None] * xs_h.astype(jnp.float32)).astype(x.dtype)
    out_m = _gated_group_rmsnorm(y.reshape(bsz, seq_len, SSM_D_INNER), z, ssm_norm_g) @ w_out_m
    mix = (jax.nn.sigmoid(gate_a) * out_a + jax.nn.sigmoid(gate_m) * out_m) @ w_o
    x = x + gate1 * mix
    h2 = _rmsnorm(x, norm2_g) * (1 + scale2) + shift2
    x = x + gate2 * _peer(h2, peer_wq, peer_keys, peer_u, peer_v)
    return x, new_buf_a, new_buf_m, new_h.astype(x.dtype)


def _normal(k, shape, scale):
    return jax.random.normal(k, shape, jnp.float32) * scale


def setup_inputs(seed: int = 0) -> dict:
    key = jax.random.key(seed)
    ks = iter(jax.random.split(key, 32))
    d = D_MODEL
    dt0 = jnp.exp(jax.random.uniform(next(ks), (DEPTH, SSM_HEADS), jnp.float32)
                  * (math.log(0.1) - math.log(1e-3)) + math.log(1e-3))
    return {
        'x_prompt': _normal(next(ks), (BATCH, SEQ, d), 1.0),
        'x_sample': _normal(next(ks), (DEC_BATCH, DEC_SEQ, d), 1.0),
        'c_prompt': _normal(next(ks), (BATCH, d), 1.0),
        'c_sample': _normal(next(ks), (DEC_BATCH, d), 1.0),
        'state_conv_a': _normal(next(ks), (DEPTH, DEC_BATCH, CONV_A_WIDTH - 1, CONV_A_DIM), 1.0),
        'state_conv_m': _normal(next(ks), (DEPTH, DEC_BATCH, SSM_CONV_WIDTH - 1, SSM_CONV_DIM), 1.0),
        'state_ssm': _normal(next(ks), (DEPTH, DEC_BATCH, SSM_HEADS, SSM_HEAD_DIM, SSM_STATE), 0.05),
        'w_ada': _normal(next(ks), (DEPTH, d, N_MOD * d), 0.5 * d ** -0.5),
        'b_ada': _normal(next(ks), (DEPTH, N_MOD * d), 0.01),
        'norm1_g': 1.0 + _normal(next(ks), (DEPTH, d), 0.02),
        'w_in': _normal(next(ks), (DEPTH, d, IN_COLS), d ** -0.5),
        'conv_a_w': _normal(next(ks), (DEPTH, CONV_A_WIDTH, CONV_A_DIM), CONV_A_WIDTH ** -0.5),
        'w_out_a': _normal(next(ks), (DEPTH, CONV_A_DIM, d), CONV_A_DIM ** -0.5),
        'conv_m_w': _normal(next(ks), (DEPTH, SSM_CONV_WIDTH, SSM_CONV_DIM), SSM_CONV_WIDTH ** -0.5),
        'conv_m_b': _normal(next(ks), (DEPTH, SSM_CONV_DIM), 0.01),
        'dt_bias': dt0 + jnp.log(-jnp.expm1(-dt0)),
        'a_log': jnp.log(jax.random.uniform(next(ks), (DEPTH, SSM_HEADS), jnp.float32, 1.0, 16.0)),
        'd_skip': 1.0 + _normal(next(ks), (DEPTH, SSM_HEADS), 0.02),
        'ssm_norm_g': 1.0 + _normal(next(ks), (DEPTH, SSM_D_INNER), 0.02),
        'w_out_m': _normal(next(ks), (DEPTH, SSM_D_INNER, d), SSM_D_INNER ** -0.5),
        'w_o': _normal(next(ks), (DEPTH, d, d), d ** -0.5),
        'norm2_g': 1.0 + _normal(next(ks), (DEPTH, d), 0.02),
        'peer_wq': _normal(next(ks), (DEPTH, d, PEER_HEADS * PEER_D_KEY), d ** -0.5),
        'peer_keys': _normal(next(ks), (DEPTH, PEER_HEADS, 2, PEER_N_KEYS, PEER_D_HALF), PEER_D_HALF ** -0.5),
        'peer_u': _normal(next(ks), (DEPTH, PEER_EXPERTS, d), d ** -0.5),
        'peer_v': _normal(next(ks), (DEPTH, PEER_EXPERTS, d), 0.5),
        'final_g': 1.0 + _normal(next(ks), (d,), 0.02),
    }


def reference(x_prompt, x_sample, c_prompt, c_sample, state_conv_a, state_conv_m, state_ssm,
              w_ada, b_ada, norm1_g, w_in, conv_a_w, w_out_a, conv_m_w, conv_m_b, dt_bias, a_log, d_skip,
              ssm_norm_g, w_out_m, w_o, norm2_g, peer_wq, peer_keys, peer_u, peer_v, final_g):
    bp = x_prompt.shape[0]
    fdt = x_prompt.dtype
    zero_a = jnp.zeros((bp, CONV_A_WIDTH - 1, CONV_A_DIM), fdt)
    zero_m = jnp.zeros((bp, SSM_CONV_WIDTH - 1, SSM_CONV_DIM), fdt)
    zero_h = jnp.zeros((bp, SSM_HEADS, SSM_HEAD_DIM, SSM_STATE), fdt)
    xp, xs = x_prompt, x_sample
    pa_l, pm_l, ph_l, sa_l, sm_l, sh_l = [], [], [], [], [], []
    for l in range(DEPTH):
        lw = (w_ada[l], b_ada[l], norm1_g[l], w_in[l], conv_a_w[l], w_out_a[l], conv_m_w[l], conv_m_b[l],
              dt_bias[l], a_log[l], d_skip[l], ssm_norm_g[l], w_out_m[l], w_o[l], norm2_g[l],
              peer_wq[l], peer_keys[l], peer_u[l], peer_v[l])
        xp, pa, pm, ph = _layer(xp, c_prompt, zero_a, zero_m, zero_h, *lw)
        xs, sa, sm, sh = _layer(xs, c_sample, state_conv_a[l], state_conv_m[l], state_ssm[l], *lw)
        pa_l.append(pa); pm_l.append(pm); ph_l.append(ph)
        sa_l.append(sa); sm_l.append(sm); sh_l.append(sh)
    y_prompt = _rmsnorm(xp, final_g)
    y_sample = _rmsnorm(xs, final_g)
    return (y_prompt, y_sample, jnp.stack(pa_l), jnp.stack(pm_l), jnp.stack(ph_l),
            jnp.stack(sa_l), jnp.stack(sm_l), jnp.stack(sh_l))
```

```python
import functools
import math

import jax
import jax.numpy as jnp
from jax import lax
from jax.experimental import pallas as pl
from jax.experimental.pallas import tpu as pltpu

F32 = jnp.float32
BF16 = jnp.bfloat16
NORM_EPS = 1e-6
PEER_TOPK = 16
SSD_CHUNK = 128
N_MOD = 6
LANES = 128
SUBLANES = 8
VMEM_LIMIT = 56 * 1024 * 1024
HI = lax.Precision.HIGHEST
NT_DIMS = (((1,), (1,)), ((), ()))
TN_DIMS = (((0,), (0,)), ((), ()))


def _cp(*sem):
    return pltpu.CompilerParams(dimension_semantics=sem, vmem_limit_bytes=VMEM_LIMIT)


def _tile(n, pref, mult=LANES):
    if n <= pref:
        return n
    t = (pref // mult) * mult
    while t >= mult:
        if n % t == 0:
            return t
        t -= mult
    return n


class _Group:
    def __init__(self, n_seq, seq_len, mod):
        self.n_seq, self.seq_len, self.mod = n_seq, seq_len, mod
        self.tokens = n_seq * seq_len
        self.per_token = seq_len == 1

    def row_tile(self, pref):
        if self.per_token:
            return _tile(self.tokens, pref, SUBLANES)
        return _tile(self.seq_len, pref, SUBLANES)

    def mod_spec(self, layer, k, tm, tn=None, col=False):
        d = self.mod.shape[-1]
        tn_ = d if tn is None else tn
        if self.per_token:
            if col:
                return pl.BlockSpec((None, None, tm, tn_), lambda i, j, *_: (layer, k, i, j))
            return pl.BlockSpec((None, None, tm, tn_), lambda i, *_: (layer, k, i, 0))
        sl = self.seq_len
        if col:
            return pl.BlockSpec((None, None, None, 1, tn_), lambda i, j, *_: (layer, (i * tm) // sl, k, 0, j))
        return pl.BlockSpec((None, None, None, 1, tn_), lambda i, *_: (layer, (i * tm) // sl, k, 0, 0))


def _ada_kernel(c_ref, w_ref, b_ref, o_ref):
    sc = jax.nn.silu(c_ref[...]).astype(BF16)
    o_ref[...] = jnp.dot(sc, w_ref[...].astype(BF16), preferred_element_type=F32) + b_ref[...]


def _ada(c_all, w_ada, b_ada):
    depth, d, n = w_ada.shape
    bc = c_all.shape[0]
    tn = _tile(n, 1024)
    return pl.pallas_call(
        _ada_kernel, grid=(depth, n // tn),
        in_specs=[pl.BlockSpec((bc, d), lambda l, j: (0, 0)),
                  pl.BlockSpec((None, d, tn), lambda l, j: (l, 0, j)),
                  pl.BlockSpec((None, 1, tn), lambda l, j: (l, 0, j))],
        out_specs=pl.BlockSpec((None, bc, tn), lambda l, j: (l, 0, j)),
        out_shape=jax.ShapeDtypeStruct((depth, bc, n), F32),
        compiler_params=_cp("parallel", "parallel"), name="ada",
    )(c_all, w_ada, b_ada.reshape(depth, 1, n))


def _norm_kernel(has_add, has_mod, *refs):
    refs = list(refs)
    x_ref = refs.pop(0)
    x = x_ref[...]
    if has_add:
        add_ref, gate_ref = refs.pop(0), refs.pop(0)
        x = x + gate_ref[...] * add_ref[...]
    g_ref = refs.pop(0)
    if has_mod:
        sc_ref, sh_ref = refs.pop(0), refs.pop(0)
    if has_add and has_mod:
        xo_ref = refs.pop(0)
        xo_ref[...] = x
    h_ref = refs.pop(0)
    xn = x * lax.rsqrt(jnp.mean(x * x, axis=-1, keepdims=True) + NORM_EPS) * g_ref[...]
    if has_mod:
        xn = xn * (1.0 + sc_ref[...]) + sh_ref[...]
    h_ref[...] = xn.astype(h_ref.dtype)


def _norm(grp, layer, x, g, g_layer, add=None, gate_k=None, gate_layer=None, mod_k=None):
    t, d = x.shape
    tm = grp.row_tile(256)
    has_add, has_mod = add is not None, mod_k is not None
    row = pl.BlockSpec((tm, d), lambda i: (i, 0))
    ins, specs = [x], [row]
    if has_add:
        ins += [add, grp.mod]
        specs += [row, grp.mod_spec(gate_layer, gate_k, tm)]
    ins.append(g)
    specs.append(pl.BlockSpec((None, 1, d), lambda i: (g_layer, 0, 0)))
    if has_mod:
        ins += [grp.mod, grp.mod]
        specs += [grp.mod_spec(layer, mod_k[0], tm), grp.mod_spec(layer, mod_k[1], tm)]
    out_shapes, out_specs = [], []
    if has_add and has_mod:
        out_shapes.append(jax.ShapeDtypeStruct((t, d), F32))
        out_specs.append(row)
    out_shapes.append(jax.ShapeDtypeStruct((t, d), BF16 if has_mod else F32))
    out_specs.append(row)
    res = pl.pallas_call(
        functools.partial(_norm_kernel, has_add, has_mod), grid=(t // tm,),
        in_specs=specs, out_specs=out_specs, out_shape=out_shapes,
        compiler_params=_cp("parallel"), name="norm",
    )(*ins)
    return res if len(res) > 1 else res[0]


def _mm_kernel(x_ref, w_ref, o_ref):
    o_ref[...] = jnp.dot(x_ref[...], w_ref[...].astype(BF16), preferred_element_type=F32).astype(o_ref.dtype)


def _mm(x, w, layer, tm, tn, name):
    m, k = x.shape
    n = w.shape[2]
    return pl.pallas_call(
        _mm_kernel, grid=(m // tm, pl.cdiv(n, tn)),
        in_specs=[pl.BlockSpec((tm, k), lambda i, j: (i, 0)),
                  pl.BlockSpec((None, k, tn), lambda i, j: (layer, 0, j))],
        out_specs=pl.BlockSpec((tm, tn), lambda i, j: (i, j)),
        out_shape=jax.ShapeDtypeStruct((m, n), F32),
        compiler_params=_cp("parallel", "parallel"), name=name,
    )(x, w)


def _merge_kernel(ga_ref, wa_ref, gm_ref, wm_ref, sa_ref, sm_ref, o_ref):
    out_a = jnp.dot(ga_ref[...], wa_ref[...].astype(BF16), preferred_element_type=F32)
    out_m = jnp.dot(gm_ref[...], wm_ref[...].astype(BF16), preferred_element_type=F32)
    o_ref[...] = (jax.nn.sigmoid(sa_ref[...]) * out_a + jax.nn.sigmoid(sm_ref[...]) * out_m).astype(o_ref.dtype)


def _merge(ga, gm, proj, w_out_a, w_out_m, layer, d, tm, tn):
    m, ca = ga.shape
    di = gm.shape[1]
    nj = d // tn
    return pl.pallas_call(
        _merge_kernel, grid=(m // tm, nj),
        in_specs=[pl.BlockSpec((tm, ca), lambda i, j: (i, 0)),
                  pl.BlockSpec((None, ca, tn), lambda i, j: (layer, 0, j)),
                  pl.BlockSpec((tm, di), lambda i, j: (i, 0)),
                  pl.BlockSpec((None, di, tn), lambda i, j: (layer, 0, j)),
                  pl.BlockSpec((tm, tn), lambda i, j: (i, j)),
                  pl.BlockSpec((tm, tn), lambda i, j: (i, nj + j))],
        out_specs=pl.BlockSpec((tm, tn), lambda i, j: (i, j)),
        out_shape=jax.ShapeDtypeStruct((m, d), BF16),
        compiler_params=_cp("parallel", "parallel"), name="merge",
    )(ga, w_out_a, gm, w_out_m, proj, proj)


def _resid_kernel(mix_ref, w_ref, x_ref, gate_ref, o_ref):
    y = jnp.dot(mix_ref[...], w_ref[...].astype(BF16), preferred_element_type=F32)
    o_ref[...] = x_ref[...] + gate_ref[...] * y


def _resid_mm(grp, layer, mix, w_o, x, gate_k, tm, tn):
    m, k = mix.shape
    d = x.shape[1]
    return pl.pallas_call(
        _resid_kernel, grid=(m // tm, d // tn),
        in_specs=[pl.BlockSpec((tm, k), lambda i, j: (i, 0)),
                  pl.BlockSpec((None, k, tn), lambda i, j: (layer, 0, j)),
                  pl.BlockSpec((tm, tn), lambda i, j: (i, j)),
                  grp.mod_spec(layer, gate_k, tm, tn, col=True)],
        out_specs=pl.BlockSpec((tm, tn), lambda i, j: (i, j)),
        out_shape=jax.ShapeDtypeStruct((m, d), F32),
        compiler_params=_cp("parallel", "parallel"), name="resid_mm",
    )(mix, w_o, x, grp.mod)


def _causal_conv(u, prev, w):
    width = w.shape[0]
    row = lax.broadcasted_iota(jnp.int32, u.shape, 0)
    y = w[width - 1:width] * u
    for s in range(1, width):
        sh = pltpu.roll(u, s, 0)
        for r in range(s):
            sh = jnp.where(row == r, prev[width - 1 + r - s], sh)
        y = y + w[width - 1 - s:width - s] * sh
    return y


def _conv_a_kernel(ts, ab_ref, ac_ref, ah_ref, buf_ref, w_ref, ga_ref, nb_ref, carry_ref):
    @pl.when(pl.program_id(2) == 0)
    def _():
        carry_ref[0:2, :] = buf_ref[...]

    u = ac_ref[...] * ah_ref[...]
    prev = [carry_ref[0:1, :], carry_ref[1:2, :]]
    conv = _causal_conv(u, prev, w_ref[...])
    ga_ref[...] = (ab_ref[...] * conv).astype(ga_ref.dtype)
    last = ac_ref[ts - 2:ts, :] * ah_ref[ts - 2:ts, :]
    carry_ref[0:2, :] = last
    nb_ref[...] = last


def _conv_a_prompt(proj, buf, conv_w, layer, n_seq, seq_len, col0, ca):
    wc = _tile(ca, 2048)
    assert col0 % wc == 0
    ts = _tile(seq_len, 256, SUBLANES)
    nt = seq_len // ts
    ncol = ca // wc
    cb = col0 // wc

    def col_spec(k):
        return pl.BlockSpec((ts, wc), lambda b, c, i: (b * nt + i, cb + k * ncol + c))

    return pl.pallas_call(
        functools.partial(_conv_a_kernel, ts), grid=(n_seq, ncol, nt),
        in_specs=[col_spec(0), col_spec(1), col_spec(2),
                  pl.BlockSpec((None, 2, wc), lambda b, c, i: (b, 0, c)),
                  pl.BlockSpec((None, 3, wc), lambda b, c, i: (layer, 0, c))],
        out_specs=[pl.BlockSpec((ts, wc), lambda b, c, i: (b * nt + i, c)),
                   pl.BlockSpec((None, 2, wc), lambda b, c, i: (b, 0, c))],
        out_shape=[jax.ShapeDtypeStruct((n_seq * seq_len, ca), BF16),
                   jax.ShapeDtypeStruct((n_seq, 2, ca), F32)],
        scratch_shapes=[pltpu.VMEM((SUBLANES, wc), F32)],
        compiler_params=_cp("parallel", "parallel", "arbitrary"), name="conv_a",
    )(proj, proj, proj, buf, conv_w)


def _conv_m_kernel(ts, x_ref, buf_ref, w_ref, b_ref, o_ref, nb_ref, carry_ref):
    @pl.when(pl.program_id(2) == 0)
    def _():
        carry_ref[0:3, :] = buf_ref[...]

    u = x_ref[...]
    prev = [carry_ref[0:1, :], carry_ref[1:2, :], carry_ref[2:3, :]]
    o_ref[...] = jax.nn.silu(_causal_conv(u, prev, w_ref[...]) + b_ref[...])
    last = x_ref[ts - 3:ts, :]
    carry_ref[0:3, :] = last
    nb_ref[...] = last


def _conv_m_prompt(proj, buf, conv_w, conv_b, layer, n_seq, seq_len, col0, cm):
    wc = _tile(math.gcd(col0, cm), 2048)
    assert col0 % wc == 0 and cm % wc == 0
    ts = _tile(seq_len, 256, SUBLANES)
    nt = seq_len // ts
    cb = col0 // wc
    depth = conv_w.shape[0]
    return pl.pallas_call(
        functools.partial(_conv_m_kernel, ts), grid=(n_seq, cm // wc, nt),
        in_specs=[pl.BlockSpec((ts, wc), lambda b, c, i: (b * nt + i, cb + c)),
                  pl.BlockSpec((None, 3, wc), lambda b, c, i: (b, 0, c)),
                  pl.BlockSpec((None, 4, wc), lambda b, c, i: (layer, 0, c)),
                  pl.BlockSpec((None, 1, wc), lambda b, c, i: (layer, 0, c))],
        out_specs=[pl.BlockSpec((ts, wc), lambda b, c, i: (b * nt + i, c)),
                   pl.BlockSpec((None, 3, wc), lambda b, c, i: (b, 0, c))],
        out_shape=[jax.ShapeDtypeStruct((n_seq * seq_len, cm), F32),
                   jax.ShapeDtypeStruct((n_seq, 3, cm), F32)],
        scratch_shapes=[pltpu.VMEM((SUBLANES, wc), F32)],
        compiler_params=_cp("parallel", "parallel", "arbitrary"), name="conv_m",
    )(proj, buf, conv_w, conv_b.reshape(depth, 1, cm))


def _conv_a_step_kernel(ab_ref, ac_ref, ah_ref, b0_ref, b1_ref, w_ref, ga_ref, n0_ref, n1_ref):
    u = ac_ref[...] * ah_ref[...]
    w = w_ref[...]
    conv = w[0:1] * b0_ref[...] + w[1:2] * b1_ref[...] + w[2:3] * u
    ga_ref[...] = (ab_ref[...] * conv).astype(ga_ref.dtype)
    n0_ref[...] = b1_ref[...]
    n1_ref[...] = u


def _conv_a_step(proj, buf2, conv_w, layer, col0, ca):
    n = proj.shape[0]
    wc = _tile(ca, 2048)
    ncol = ca // wc
    cb = col0 // wc

    def col_spec(k):
        return pl.BlockSpec((n, wc), lambda c: (0, cb + k * ncol + c))

    def buf_spec(k):
        return pl.BlockSpec((None, n, wc), lambda c: (layer, 0, k * ncol + c))

    ga, n0, n1 = pl.pallas_call(
        _conv_a_step_kernel, grid=(ncol,),
        in_specs=[col_spec(0), col_spec(1), col_spec(2), buf_spec(0), buf_spec(1),
                  pl.BlockSpec((None, 3, wc), lambda c: (layer, 0, c))],
        out_specs=[pl.BlockSpec((n, wc), lambda c: (0, c))] * 3,
        out_shape=[jax.ShapeDtypeStruct((n, ca), BF16), jax.ShapeDtypeStruct((n, ca), F32),
                   jax.ShapeDtypeStruct((n, ca), F32)],
        compiler_params=_cp("parallel"), name="conv_a_step",
    )(proj, proj, proj, buf2, buf2, conv_w)
    return ga, jnp.stack([n0, n1], axis=1)


def _conv_m_step_kernel(x_ref, b0_ref, b1_ref, b2_ref, w_ref, b_ref, o_ref, n0_ref, n1_ref, n2_ref):
    x = x_ref[...]
    w = w_ref[...]
    conv = w[0:1] * b0_ref[...] + w[1:2] * b1_ref[...] + w[2:3] * b2_ref[...] + w[3:4] * x + b_ref[...]
    o_ref[...] = jax.nn.silu(conv)
    n0_ref[...] = b1_ref[...]
    n1_ref[...] = b2_ref[...]
    n2_ref[...] = x


def _conv_m_step(proj, buf3, conv_w, conv_b, layer, col0, cm):
    n = proj.shape[0]
    wc = _tile(math.gcd(col0, cm), 2048)
    ncol = cm // wc
    cb = col0 // wc
    depth = conv_w.shape[0]

    def buf_spec(k):
        return pl.BlockSpec((None, n, wc), lambda c: (layer, 0, k * ncol + c))

    act, n0, n1, n2 = pl.pallas_call(
        _conv_m_step_kernel, grid=(ncol,),
        in_specs=[pl.BlockSpec((n, wc), lambda c: (0, cb + c)), buf_spec(0), buf_spec(1), buf_spec(2),
                  pl.BlockSpec((None, 4, wc), lambda c: (layer, 0, c)),
                  pl.BlockSpec((None, 1, wc), lambda c: (layer, 0, c))],
        out_specs=[pl.BlockSpec((n, wc), lambda c: (0, c))] * 4,
        out_shape=[jax.ShapeDtypeStruct((n, cm), F32)] * 4,
        compiler_params=_cp("parallel"), name="conv_m_step",
    )(proj, buf3, buf3, buf3, conv_w, conv_b.reshape(depth, 1, cm))
    return act, jnp.stack([n0, n1, n2], axis=1)


def _softplus(x):
    return jnp.maximum(x, 0.0) + jnp.log1p(jnp.exp(-jnp.abs(x)))


def _gated_group_norm(y, z, gn, n_groups):
    yz = y * jax.nn.silu(z)
    gw = y.shape[1] // n_groups
    outs = []
    for g in range(n_groups):
        blk = yz[:, g * gw:(g + 1) * gw]
        ms = jnp.mean(blk * blk, axis=-1, keepdims=True)
        outs.append(blk * lax.rsqrt(ms + NORM_EPS) * gn[:, g * gw:(g + 1) * gw])
    return jnp.concatenate(outs, axis=1)


def _ssd_kernel(n_heads, hd, ns, n_groups, nz, *refs):
    xs_ref, bm_ref, cm_ref = refs[0:3]
    z_refs = refs[3:3 + nz]
    dt_ref, dtb_ref, alog_ref, alogrep_ref, drep_ref, gn_ref, rexp_ref, y_ref, st_ref, s_ref = refs[3 + nz:]
    q = xs_ref.shape[0]
    r = n_heads // n_groups
    gw = r * hd

    @pl.when(pl.program_id(1) == 0)
    def _():
        s_ref[...] = jnp.zeros_like(s_ref)

    lane = lax.broadcasted_iota(jnp.int32, (q, LANES), 1)
    dt = jnp.where(lane < n_heads, _softplus(dt_ref[...] + dtb_ref[...]), 0.0)
    da = dt * (-jnp.exp(alog_ref[...]))
    rowi = lax.broadcasted_iota(jnp.int32, (q, q), 0)
    coli = lax.broadcasted_iota(jnp.int32, (q, q), 1)
    causal = rowi >= coli
    tri = causal.astype(F32)
    acs = jnp.dot(tri, da, precision=HI, preferred_element_type=F32)
    acs_t = acs.T
    rexp = rexp_ref[...]
    dt_rep = jnp.dot(dt, rexp, precision=HI, preferred_element_type=F32)
    acs_rep = jnp.dot(acs, rexp, precision=HI, preferred_element_type=F32)
    xs = xs_ref[...]
    xdt = xs * dt_rep
    last = acs_rep[q - 1:q, :]
    xw = (xdt * jnp.exp(last - acs_rep)).astype(BF16)
    xdt_b = xdt.astype(BF16)
    eacs = jnp.exp(acs_rep)
    lane_q = lax.broadcasted_iota(jnp.int32, (q, LANES), 1)
    heads_per_blk = LANES // hd
    y_groups = []
    for g in range(n_groups):
        bg = bm_ref[:, g * ns:(g + 1) * ns].astype(BF16)
        cg = cm_ref[:, g * ns:(g + 1) * ns].astype(BF16)
        cb = lax.dot_general(cg, bg, NT_DIMS, preferred_element_type=F32)
        sg = s_ref[g * gw:(g + 1) * gw, :]
        y_off = lax.dot_general(cg, sg.astype(BF16), NT_DIMS, preferred_element_type=F32)
        blks = []
        for jb in range(gw // LANES):
            off = g * gw + jb * LANES
            xblk = xdt_b[:, off:off + LANES]
            yd = jnp.zeros((q, LANES), F32)
            for hh in range(heads_per_blk):
                h = off // hd + hh
                seg = acs[:, h:h + 1] - acs_t[h:h + 1, :]
                m = (cb * jnp.where(causal, jnp.exp(seg), 0.0)).astype(BF16)
                xm = jnp.where((lane_q >= hh * hd) & (lane_q < (hh + 1) * hd), xblk, jnp.zeros_like(xblk))
                yd = yd + jnp.dot(m, xm, preferred_element_type=F32)
            blks.append(yd)
        y_diag = jnp.concatenate(blks, axis=1)
        cols = slice(g * gw, (g + 1) * gw)
        y_groups.append(y_diag + y_off * eacs[:, cols] + drep_ref[:, cols] * xs[:, cols])
        contrib = lax.dot_general(xw[:, cols], bg, TN_DIMS, preferred_element_type=F32)
        decay_col = jnp.exp(jnp.broadcast_to(last[:, cols], (LANES, gw)).T)
        s_ref[g * gw:(g + 1) * gw, :] = decay_col[:, :ns] * sg + contrib
    y = jnp.concatenate(y_groups, axis=1)
    z = jnp.concatenate([zr[...] for zr in z_refs], axis=1)
    y_ref[...] = _gated_group_norm(y, z, gn_ref[...], n_groups).astype(y_ref.dtype)
    st_ref[...] = s_ref[...]


def _ssd_consts(dt_bias, a_log, d_skip, hd):
    depth, n_heads = dt_bias.shape
    pad = lambda t: jnp.pad(t, ((0, 0), (0, LANES - n_heads))).reshape(depth, 1, LANES)
    rep = lambda t: jnp.repeat(t, hd, axis=1).reshape(depth, 1, n_heads * hd)
    rexp = jnp.repeat(jnp.eye(LANES, n_heads, dtype=F32), hd, axis=1)
    return pad(dt_bias), pad(a_log), rep(a_log), rep(d_skip), rexp


def _ssd_prompt(act, proj, consts, gn, layer, n_seq, seq_len, dims, z_col0, dt_col0, zw):
    n_heads, hd, ns, n_groups = dims
    di, gn_w = n_heads * hd, n_groups * ns
    q = min(SSD_CHUNK, seq_len)
    assert seq_len % q == 0 and n_heads <= LANES and ns == LANES and LANES % hd == 0
    assert di % gn_w == 0 and z_col0 % zw == 0 and di % zw == 0 and dt_col0 % LANES == 0
    nc = seq_len // q
    nz = di // zw
    dtb, alog, alog_rep, d_rep, rexp = consts
    depth = gn.shape[0]
    row = lambda b, c: b * nc + c
    vec = lambda w: pl.BlockSpec((None, 1, w), lambda b, c: (layer, 0, 0))
    z_specs = [pl.BlockSpec((q, zw), lambda b, c, k=k: (row(b, c), z_col0 // zw + k)) for k in range(nz)]
    return pl.pallas_call(
        functools.partial(_ssd_kernel, n_heads, hd, ns, n_groups, nz), grid=(n_seq, nc),
        in_specs=[pl.BlockSpec((q, di), lambda b, c: (row(b, c), 0)),
                  pl.BlockSpec((q, gn_w), lambda b, c: (row(b, c), di // gn_w)),
                  pl.BlockSpec((q, gn_w), lambda b, c: (row(b, c), di // gn_w + 1))]
        + z_specs
        + [pl.BlockSpec((q, LANES), lambda b, c: (row(b, c), dt_col0 // LANES)),
           vec(LANES), vec(LANES), vec(di), vec(di), vec(di),
           pl.BlockSpec((LANES, di), lambda b, c: (0, 0))],
        out_specs=[pl.BlockSpec((q, di), lambda b, c: (row(b, c), 0)),
                   pl.BlockSpec((None, di, ns), lambda b, c: (b, 0, 0))],
        out_shape=[jax.ShapeDtypeStruct((n_seq * seq_len, di), BF16),
                   jax.ShapeDtypeStruct((n_seq, di, ns), F32)],
        scratch_shapes=[pltpu.VMEM((di, ns), F32)],
        compiler_params=_cp("parallel", "arbitrary"), name="ssd",
    )(act, act, act, *([proj] * nz), proj, dtb, alog, alog_rep, d_rep, gn.reshape(depth, 1, di), rexp)


def _ssd_step_kernel(n_heads, hd, ns, n_groups, nz, *refs):
    xs_ref, bm_ref, cm_ref = refs[0:3]
    z_refs = refs[3:3 + nz]
    (dt_ref, dtb_ref, alogrep_ref, drep_ref, gn_ref, rexp_ref, sin_ref,
     y_ref, sout_ref, dtx_t, dec_t, y_t) = refs[3 + nz:]
    b = pl.program_id(0)
    nb = xs_ref.shape[0]
    gw = (n_heads // n_groups) * hd

    @pl.when(b == 0)
    def _():
        lane = lax.broadcasted_iota(jnp.int32, (nb, LANES), 1)
        dt = jnp.where(lane < n_heads, _softplus(dt_ref[...] + dtb_ref[...]), 0.0)
        dt_rep = jnp.dot(dt, rexp_ref[...], precision=HI, preferred_element_type=F32)
        dtx_t[...] = (xs_ref[...] * dt_rep).T
        dec_t[...] = jnp.exp(dt_rep * (-jnp.exp(alogrep_ref[...]))).T
        y_t[...] = jnp.zeros_like(y_t)

    sel = lax.broadcasted_iota(jnp.int32, (gw, nb), 1) == b
    bm_row = bm_ref[pl.ds(b, 1), :]
    cm_row = cm_ref[pl.ds(b, 1), :]
    for g in range(n_groups):
        rows = slice(g * gw, (g + 1) * gw)
        xcol = jnp.sum(jnp.where(sel, dtx_t[rows, :], 0.0), axis=1, keepdims=True)
        dcol = jnp.sum(jnp.where(sel, dec_t[rows, :], 0.0), axis=1, keepdims=True)
        brow = bm_row[:, g * ns:(g + 1) * ns]
        crow = cm_row[:, g * ns:(g + 1) * ns]
        s_new = dcol * sin_ref[rows, :] + xcol * brow
        sout_ref[rows, :] = s_new
        ycol = jnp.sum(s_new * crow, axis=1, keepdims=True)
        y_t[rows, :] = jnp.where(sel, ycol, y_t[rows, :])

    @pl.when(b == nb - 1)
    def _():
        y = y_t[...].T + drep_ref[...] * xs_ref[...]
        z = jnp.concatenate([zr[...] for zr in z_refs], axis=1)
        y_ref[...] = _gated_group_norm(y, z, gn_ref[...], n_groups).astype(y_ref.dtype)


def _ssd_step(act, proj, consts, gn, state, layer, dims, z_col0, dt_col0, zw):
    n_heads, hd, ns, n_groups = dims
    di, gn_w = n_heads * hd, n_groups * ns
    n = act.shape[0]
    assert n % LANES == 0 and ns == LANES
    nz = di // zw
    dtb, _, alog_rep, d_rep, rexp = consts
    depth = gn.shape[0]
    vec = lambda w: pl.BlockSpec((None, 1, w), lambda b: (layer, 0, 0))
    z_specs = [pl.BlockSpec((n, zw), lambda b, k=k: (0, z_col0 // zw + k)) for k in range(nz)]
    return pl.pallas_call(
        functools.partial(_ssd_step_kernel, n_heads, hd, ns, n_groups, nz), grid=(n,),
        in_specs=[pl.BlockSpec((n, di), lambda b: (0, 0)),
                  pl.BlockSpec((n, gn_w), lambda b: (0, di // gn_w)),
                  pl.BlockSpec((n, gn_w), lambda b: (0, di // gn_w + 1))]
        + z_specs
        + [pl.BlockSpec((n, LANES), lambda b: (0, dt_col0 // LANES)),
           vec(LANES), vec(di), vec(di), vec(di),
           pl.BlockSpec((LANES, di), lambda b: (0, 0)),
           pl.BlockSpec((None, None, di, ns), lambda b: (layer, b, 0, 0))],
        out_specs=[pl.BlockSpec((n, di), lambda b: (0, 0)),
                   pl.BlockSpec((None, di, ns), lambda b: (b, 0, 0))],
        out_shape=[jax.ShapeDtypeStruct((n, di), BF16),
                   jax.ShapeDtypeStruct((n, di, ns), F32)],
        scratch_shapes=[pltpu.VMEM((di, n), F32)] * 3,
        compiler_params=_cp("arbitrary"), name="ssd_step",
    )(act, act, act, *([proj] * nz), proj, dtb, alog_rep, d_rep, gn.reshape(depth, 1, di), rexp, state)


def _peer_pairs():
    k1 = PEER_TOPK + 1
    return [(i, j) for i in range(k1) for j in range(k1) if (i + 1) * (j + 1) <= k1]


def _top_rows(x, count):
    vals = []
    for it in range(count):
        m = jnp.max(x, axis=0, keepdims=True)
        vals.append(m)
        if it + 1 < count:
            x = jnp.where(x == m, -jnp.inf, x)
    return vals


def _peer_select_kernel(n_heads, q_ref, k_ref, a1_ref, cq_ref, a2_ref, cand_ref):
    nk = k_ref.shape[2]
    pairs = _peer_pairs()
    npad = cand_ref.shape[0]
    cand_ref[len(pairs):npad, :] = jnp.full((npad - len(pairs), cand_ref.shape[1]), -jnp.inf, F32)
    for h in range(n_heads):
        s = []
        for half in range(2):
            qh = q_ref[:, (2 * h + half) * nk:(2 * h + half + 1) * nk].astype(BF16)
            s.append(lax.dot_general(k_ref[h, half].astype(BF16), qh, NT_DIMS, preferred_element_type=F32))
        top1 = _top_rows(s[0], PEER_TOPK + 1)
        top2 = _top_rows(s[1], PEER_TOPK + 1)
        for idx, (i, j) in enumerate(pairs):
            cand_ref[idx:idx + 1, :] = top1[i] + top2[j]
        cand = cand_ref[...]
        best = _top_rows(cand, PEER_TOPK + 1)
        cut = 0.5 * (best[PEER_TOPK - 1] + best[PEER_TOPK])
        c0 = top1[0] + top2[0]
        zsum = jnp.sum(jnp.where(cand > cut, jnp.exp(cand - c0), 0.0), axis=0, keepdims=True)
        a1_ref[h] = jnp.exp(s[0] - top1[0]) / zsum
        a2_ref[h] = jnp.exp(s[1] - top2[0])
        cq_ref[h] = jnp.exp((cut - top2[0]) - s[0])


def _peer_select(qv, keys, layer, tm):
    t = qv.shape[0]
    _, n_heads, _, nk, dh = keys.shape
    assert nk == LANES and dh == LANES
    npad = -(-len(_peer_pairs()) // SUBLANES) * SUBLANES
    out = jax.ShapeDtypeStruct((n_heads, nk, t), F32)
    ospec = pl.BlockSpec((n_heads, nk, tm), lambda i: (0, 0, i))
    return pl.pallas_call(
        functools.partial(_peer_select_kernel, n_heads), grid=(t // tm,),
        in_specs=[pl.BlockSpec((tm, qv.shape[1]), lambda i: (i, 0)),
                  pl.BlockSpec((None, n_heads, 2, nk, dh), lambda i: (layer, 0, 0, 0, 0))],
        out_specs=[ospec, ospec, ospec], out_shape=[out, out, out],
        scratch_shapes=[pltpu.VMEM((npad, tm), F32)],
        compiler_params=_cp("parallel"), name="peer_select",
    )(qv, keys)


def _gelu(x):
    return 0.5 * x * (1.0 + lax.erf(x * math.sqrt(0.5)))


def _peer_main_kernel(n_heads, nk, h_ref, u_ref, v_ref, a1_ref, cq_ref, a2_ref, o_ref):
    e = pl.program_id(1)
    te = u_ref.shape[0]

    @pl.when(e == 0)
    def _():
        o_ref[...] = jnp.zeros_like(o_ref)

    st = lax.dot_general(u_ref[...], h_ref[...], NT_DIMS, preferred_element_type=F32)
    parts = []
    for k in range(te // nk):
        e1 = e * (te // nk) + k
        w = None
        for h in range(n_heads):
            a2 = a2_ref[h]
            term = jnp.where(a2 >= cq_ref[h, pl.ds(e1, 1), :], a2, 0.0) * a1_ref[h, pl.ds(e1, 1), :]
            w = term if w is None else w + term
        parts.append((w * _gelu(st[k * nk:(k + 1) * nk, :])).astype(BF16))
    pt = jnp.concatenate(parts, axis=0)
    o_ref[...] += lax.dot_general(pt, v_ref[...], TN_DIMS, preferred_element_type=F32)


def _peer_main(h2, u_b, v_b, a1, cq, a2, layer, tm, te):
    t, d = h2.shape
    n_exp = u_b.shape[1]
    n_heads, nk, _ = a1.shape
    fac = pl.BlockSpec((n_heads, nk, tm), lambda i, e: (0, 0, i))
    return pl.pallas_call(
        functools.partial(_peer_main_kernel, n_heads, nk), grid=(t // tm, n_exp // te),
        in_specs=[pl.BlockSpec((tm, d), lambda i, e: (i, 0)),
                  pl.BlockSpec((None, te, d), lambda i, e: (layer, e, 0)),
                  pl.BlockSpec((None, te, d), lambda i, e: (layer, e, 0)),
                  fac, fac, fac],
        out_specs=pl.BlockSpec((tm, d), lambda i, e: (i, 0)),
        out_shape=jax.ShapeDtypeStruct((t, d), F32),
        compiler_params=_cp("parallel", "arbitrary"), name="peer_main",
    )(h2, u_b, v_b, a1, cq, a2)


def kernel(x_prompt, x_sample, c_prompt, c_sample, state_conv_a, state_conv_m, state_ssm, w_ada, b_ada, norm1_g, w_in, conv_a_w, w_out_a, conv_m_w, conv_m_b, dt_bias, a_log, d_skip, ssm_norm_g, w_out_m, w_o, norm2_g, peer_wq, peer_keys, peer_u, peer_v, final_g):
    bp, seq, d = x_prompt.shape
    bs = x_sample.shape[0]
    depth = w_ada.shape[0]
    ca, cm = conv_a_w.shape[2], conv_m_w.shape[2]
    n_heads, hd, ns = dt_bias.shape[1], state_ssm.shape[3], state_ssm.shape[4]
    di = n_heads * hd
    n_groups = (cm - di) // (2 * ns)
    dims = (n_heads, hd, ns, n_groups)
    col_b = 2 * d
    col_z = col_b + 3 * ca
    col_x = col_z + di
    col_dt = col_x + cm
    zw = _tile(math.gcd(col_z, di), 2048)

    mod = _ada(jnp.concatenate([c_prompt, c_sample], axis=0), w_ada, b_ada).reshape(depth, bp + bs, N_MOD, d)
    grp_p = _Group(bp, seq, mod[:, :bp].reshape(depth, bp, N_MOD, 1, d))
    grp_s = _Group(bs, 1, jnp.transpose(mod[:, bp:], (0, 2, 1, 3)))

    consts = _ssd_consts(dt_bias, a_log, d_skip, hd)
    norm1 = norm1_g.reshape(depth, 1, d)
    norm2 = norm2_g.reshape(depth, 1, d)
    final = final_g.reshape(1, 1, d)
    u_b = peer_u.astype(BF16)
    v_b = peer_v.astype(BF16)
    st_a = state_conv_a.reshape(depth, bs, 2 * ca)
    st_m = state_conv_m.reshape(depth, bs, 3 * cm)
    st_h = state_ssm.reshape(depth, bs, di, ns)
    zero_a = jnp.zeros((bp, 2, ca), F32)
    zero_m = jnp.zeros((bp, 3, cm), F32)

    def layer_fn(grp, l, x, peer_prev):
        t = grp.tokens
        tm = grp.row_tile(1024)
        if peer_prev is None:
            h = _norm(grp, l, x, norm1, l, mod_k=(1, 0))
        else:
            x, h = _norm(grp, l, x, norm1, l, add=peer_prev, gate_k=5, gate_layer=l - 1, mod_k=(1, 0))
        proj = _mm(h, w_in, l, tm, 512, "in_proj")
        if grp.per_token:
            ga, new_a = _conv_a_step(proj, st_a, conv_a_w, l, col_b, ca)
            act, new_m = _conv_m_step(proj, st_m, conv_m_w, conv_m_b, l, col_x, cm)
            gm, new_h = _ssd_step(act, proj, consts, ssm_norm_g, st_h, l, dims, col_z, col_dt, zw)
        else:
            ga, new_a = _conv_a_prompt(proj, zero_a, conv_a_w, l, grp.n_seq, grp.seq_len, col_b, ca)
            act, new_m = _conv_m_prompt(proj, zero_m, conv_m_w, conv_m_b, l, grp.n_seq, grp.seq_len, col_x, cm)
            gm, new_h = _ssd_prompt(act, proj, consts, ssm_norm_g, l, grp.n_seq, grp.seq_len, dims,
                                    col_z, col_dt, zw)
        mix = _merge(ga, gm, proj, w_out_a, w_out_m, l, d, grp.row_tile(512), _tile(d, 512))
        x = _resid_mm(grp, l, mix, w_o, x, 2, tm, _tile(d, 512))
        h2 = _norm(grp, l, x, norm2, l, mod_k=(4, 3))
        qv = _mm(h2, peer_wq, l, tm, 512, "peer_q")
        a1, cq, a2 = _peer_select(qv, peer_keys, l, _tile(t, 256))
        peer = _peer_main(h2, u_b, v_b, a1, cq, a2, l, _tile(t, 512), 256)
        return x, peer, new_a, new_m, new_h.reshape(grp.n_seq, n_heads, hd, ns)

    xp, xs = x_prompt.reshape(bp * seq, d), x_sample.reshape(bs, d)
    pp = ps = None
    outs = [[] for _ in range(6)]
    for l in range(depth):
        xp, pp, pa, pm, ph = layer_fn(grp_p, l, xp, pp)
        xs, ps, sa, sm, sh = layer_fn(grp_s, l, xs, ps)
        for lst, val in zip(outs, (pa, pm, ph, sa, sm, sh)):
            lst.append(val)
    y_p = _norm(grp_p, depth - 1, xp, final, 0, add=pp, gate_k=5, gate_layer=depth - 1)
    y_s = _norm(grp_s, depth - 1, xs, final, 0, add=ps, gate_k=5, gate_layer=depth - 1)
    return (y_p.reshape(bp, seq, d), y_s.reshape(bs, 1, d)) + tuple(jnp.stack(o) for o in outs)
```

```python
import functools
import math

import jax
import jax.numpy as jnp
from jax import lax
from jax.experimental import pallas as pl
from jax.experimental.pallas import tpu as pltpu

F32 = jnp.float32
BF16 = jnp.bfloat16
NORM_EPS = 1e-6
PEER_TOPK = 16
SSD_CHUNK = 128
N_MOD = 6
LANES = 128
SUBLANES = 8
VMEM_LIMIT = 56 * 1024 * 1024
HI = lax.Precision.HIGHEST
NT_DIMS = (((1,), (1,)), ((), ()))
TN_DIMS = (((0,), (0,)), ((), ()))


def _cp(*sem):
    return pltpu.CompilerParams(dimension_semantics=sem, vmem_limit_bytes=VMEM_LIMIT)


def _tile(n, pref, mult=LANES):
    if n <= pref:
        return n
    t = (pref // mult) * mult
    while t >= mult:
        if n % t == 0:
            return t
        t -= mult
    return n


class _Group:
    def __init__(self, n_seq, seq_len, mod):
        self.n_seq, self.seq_len, self.mod = n_seq, seq_len, mod
        self.tokens = n_seq * seq_len
        self.per_token = seq_len == 1

    def row_tile(self, pref):
        if self.per_token:
            return _tile(self.tokens, pref, SUBLANES)
        return _tile(self.seq_len, pref, SUBLANES)

    def mod_spec(self, layer, k, tm, tn=None, col=False):
        d = self.mod.shape[-1]
        tn_ = d if tn is None else tn
        if self.per_token:
            if col:
                return pl.BlockSpec((None, None, tm, tn_), lambda i, j, *_: (layer, k, i, j))
            return pl.BlockSpec((None, None, tm, tn_), lambda i, *_: (layer, k, i, 0))
        sl = self.seq_len
        if col:
            return pl.BlockSpec((None, None, None, 1, tn_), lambda i, j, *_: (layer, (i * tm) // sl, k, 0, j))
        return pl.BlockSpec((None, None, None, 1, tn_), lambda i, *_: (layer, (i * tm) // sl, k, 0, 0))


def _ada_kernel(c_ref, w_ref, b_ref, o_ref):
    sc = jax.nn.silu(c_ref[...]).astype(BF16)
    o_ref[...] = jnp.dot(sc, w_ref[...].astype(BF16), preferred_element_type=F32) + b_ref[...]


def _ada(c_all, w_ada, b_ada):
    depth, d, n = w_ada.shape
    bc = c_all.shape[0]
    tn = _tile(n, 1024)
    return pl.pallas_call(
        _ada_kernel, grid=(depth, n // tn),
        in_specs=[pl.BlockSpec((bc, d), lambda l, j: (0, 0)),
                  pl.BlockSpec((None, d, tn), lambda l, j: (l, 0, j)),
                  pl.BlockSpec((None, 1, tn), lambda l, j: (l, 0, j))],
        out_specs=pl.BlockSpec((None, bc, tn), lambda l, j: (l, 0, j)),
        out_shape=jax.ShapeDtypeStruct((depth, bc, n), F32),
        compiler_params=_cp("parallel", "parallel"), name="ada",
    )(c_all, w_ada, b_ada.reshape(depth, 1, n))


def _norm_kernel(has_add, has_mod, *refs):
    refs = list(refs)
    x_ref = refs.pop(0)
    x = x_ref[...]
    if has_add:
        add_ref, gate_ref = refs.pop(0), refs.pop(0)
        x = x + gate_ref[...] * add_ref[...]
    g_ref = refs.pop(0)
    if has_mod:
        sc_ref, sh_ref = refs.pop(0), refs.pop(0)
    if has_add and has_mod:
        xo_ref = refs.pop(0)
        xo_ref[...] = x
    h_ref = refs.pop(0)
    xn = x * lax.rsqrt(jnp.mean(x * x, axis=-1, keepdims=True) + NORM_EPS) * g_ref[...]
    if has_mod:
        xn = xn * (1.0 + sc_ref[...]) + sh_ref[...]
    h_ref[...] = xn.astype(h_ref.dtype)


def _norm(grp, layer, x, g, g_layer, add=None, gate_k=None, gate_layer=None, mod_k=None):
    t, d = x.shape
    tm = grp.row_tile(256)
    has_add, has_mod = add is not None, mod_k is not None
    row = pl.BlockSpec((tm, d), lambda i: (i, 0))
    ins, specs = [x], [row]
    if has_add:
        ins += [add, grp.mod]
        specs += [row, grp.mod_spec(gate_layer, gate_k, tm)]
    ins.append(g)
    specs.append(pl.BlockSpec((None, 1, d), lambda i: (g_layer, 0, 0)))
    if has_mod:
        ins += [grp.mod, grp.mod]
        specs += [grp.mod_spec(layer, mod_k[0], tm), grp.mod_spec(layer, mod_k[1], tm)]
    out_shapes, out_specs = [], []
    if has_add and has_mod:
        out_shapes.append(jax.ShapeDtypeStruct((t, d), F32))
        out_specs.append(row)
    out_shapes.append(jax.ShapeDtypeStruct((t, d), BF16 if has_mod else F32))
    out_specs.append(row)
    res = pl.pallas_call(
        functools.partial(_norm_kernel, has_add, has_mod), grid=(t // tm,),
        in_specs=specs, out_specs=out_specs, out_shape=out_shapes,
        compiler_params=_cp("parallel"), name="norm",
    )(*ins)
    return res if len(res) > 1 else res[0]


def _mm_kernel(x_ref, w_ref, o_ref):
    o_ref[...] = jnp.dot(x_ref[...], w_ref[...].astype(BF16), preferred_element_type=F32).astype(o_ref.dtype)


def _mm(x, w, layer, tm, tn, name):
    m, k = x.shape
    n = w.shape[2]
    return pl.pallas_call(
        _mm_kernel, grid=(m // tm, pl.cdiv(n, tn)),
        in_specs=[pl.BlockSpec((tm, k), lambda i, j: (i, 0)),
                  pl.BlockSpec((None, k, tn), lambda i, j: (layer, 0, j))],
        out_specs=pl.BlockSpec((tm, tn), lambda i, j: (i, j)),
        out_shape=jax.ShapeDtypeStruct((m, n), F32),
        compiler_params=_cp("parallel", "parallel"), name=name,
    )(x, w)


def _merge_kernel(ga_ref, wa_ref, gm_ref, wm_ref, sa_ref, sm_ref, o_ref):
    out_a = jnp.dot(ga_ref[...], wa_ref[...].astype(BF16), preferred_element_type=F32)
    out_m = jnp.dot(gm_ref[...], wm_ref[...].astype(BF16), preferred_element_type=F32)
    o_ref[...] = (jax.nn.sigmoid(sa_ref[...]) * out_a + jax.nn.sigmoid(sm_ref[...]) * out_m).astype(o_ref.dtype)


def _merge(ga, gm, proj, w_out_a, w_out_m, layer, d, tm, tn):
    m, ca = ga.shape
    di = gm.shape[1]
    nj = d // tn
    return pl.pallas_call(
        _merge_kernel, grid=(m // tm, nj),
        in_specs=[pl.BlockSpec((tm, ca), lambda i, j: (i, 0)),
                  pl.BlockSpec((None, ca, tn), lambda i, j: (layer, 0, j)),
                  pl.BlockSpec((tm, di), lambda i, j: (i, 0)),
                  pl.BlockSpec((None, di, tn), lambda i, j: (layer, 0, j)),
                  pl.BlockSpec((tm, tn), lambda i, j: (i, j)),
                  pl.BlockSpec((tm, tn), lambda i, j: (i, nj + j))],
        out_specs=pl.BlockSpec((tm, tn), lambda i, j: (i, j)),
        out_shape=jax.ShapeDtypeStruct((m, d), BF16),
        compiler_params=_cp("parallel", "parallel"), name="merge",
    )(ga, w_out_a, gm, w_out_m, proj, proj)


def _resid_kernel(mix_ref, w_ref, x_ref, gate_ref, o_ref):
    y = jnp.dot(mix_ref[...], w_ref[...].astype(BF16), preferred_element_type=F32)
    o_ref[...] = x_ref[...] + gate_ref[...] * y


def _resid_mm(grp, layer, mix, w_o, x, gate_k, tm, tn):
    m, k = mix.shape
    d = x.shape[1]
    return pl.pallas_call(
        _resid_kernel, grid=(m // tm, d // tn),
        in_specs=[pl.BlockSpec((tm, k), lambda i, j: (i, 0)),
                  pl.BlockSpec((None, k, tn), lambda i, j: (layer, 0, j)),
                  pl.BlockSpec((tm, tn), lambda i, j: (i, j)),
                  grp.mod_spec(layer, gate_k, tm, tn, col=True)],
        out_specs=pl.BlockSpec((tm, tn), lambda i, j: (i, j)),
        out_shape=jax.ShapeDtypeStruct((m, d), F32),
        compiler_params=_cp("parallel", "parallel"), name="resid_mm",
    )(mix, w_o, x, grp.mod)


def _causal_conv(u, prev, w):
    width = w.shape[0]
    row = lax.broadcasted_iota(jnp.int32, u.shape, 0)
    y = w[width - 1:width] * u
    for s in range(1, width):
        sh = pltpu.roll(u, s, 0)
        for r in range(s):
            sh = jnp.where(row == r, prev[width - 1 + r - s], sh)
        y = y + w[width - 1 - s:width - s] * sh
    return y


def _conv_a_kernel(ts, ab_ref, ac_ref, ah_ref, buf_ref, w_ref, ga_ref, nb_ref, carry_ref):
    @pl.when(pl.program_id(2) == 0)
    def _():
        carry_ref[0:2, :] = buf_ref[...]

    u = ac_ref[...] * ah_ref[...]
    prev = [carry_ref[0:1, :], carry_ref[1:2, :]]
    conv = _causal_conv(u, prev, w_ref[...])
    ga_ref[...] = (ab_ref[...] * conv).astype(ga_ref.dtype)
    last = ac_ref[ts - 2:ts, :] * ah_ref[ts - 2:ts, :]
    carry_ref[0:2, :] = last
    nb_ref[...] = last


def _conv_a_prompt(proj, buf, conv_w, layer, n_seq, seq_len, col0, ca):
    wc = _tile(ca, 2048)
    assert col0 % wc == 0
    ts = _tile(seq_len, 256, SUBLANES)
    nt = seq_len // ts
    ncol = ca // wc
    cb = col0 // wc

    def col_spec(k):
        return pl.BlockSpec((ts, wc), lambda b, c, i: (b * nt + i, cb + k * ncol + c))

    return pl.pallas_call(
        functools.partial(_conv_a_kernel, ts), grid=(n_seq, ncol, nt),
        in_specs=[col_spec(0), col_spec(1), col_spec(2),
                  pl.BlockSpec((None, 2, wc), lambda b, c, i: (b, 0, c)),
                  pl.BlockSpec((None, 3, wc), lambda b, c, i: (layer, 0, c))],
        out_specs=[pl.BlockSpec((ts, wc), lambda b, c, i: (b * nt + i, c)),
                   pl.BlockSpec((None, 2, wc), lambda b, c, i: (b, 0, c))],
        out_shape=[jax.ShapeDtypeStruct((n_seq * seq_len, ca), BF16),
                   jax.ShapeDtypeStruct((n_seq, 2, ca), F32)],
        scratch_shapes=[pltpu.VMEM((SUBLANES, wc), F32)],
        compiler_params=_cp("parallel", "parallel", "arbitrary"), name="conv_a",
    )(proj, proj, proj, buf, conv_w)


def _conv_m_kernel(ts, x_ref, buf_ref, w_ref, b_ref, o_ref, nb_ref, carry_ref):
    @pl.when(pl.program_id(2) == 0)
    def _():
        carry_ref[0:3, :] = buf_ref[...]

    u = x_ref[...]
    prev = [carry_ref[0:1, :], carry_ref[1:2, :], carry_ref[2:3, :]]
    o_ref[...] = jax.nn.silu(_causal_conv(u, prev, w_ref[...]) + b_ref[...])
    last = x_ref[ts - 3:ts, :]
    carry_ref[0:3, :] = last
    nb_ref[...] = last


def _conv_m_prompt(proj, buf, conv_w, conv_b, layer, n_seq, seq_len, col0, cm):
    wc = _tile(math.gcd(col0, cm), 2048)
    assert col0 % wc == 0 and cm % wc == 0
    ts = _tile(seq_len, 256, SUBLANES)
    nt = seq_len // ts
    cb = col0 // wc
    depth = conv_w.shape[0]
    return pl.pallas_call(
        functools.partial(_conv_m_kernel, ts), grid=(n_seq, cm // wc, nt),
        in_specs=[pl.BlockSpec((ts, wc), lambda b, c, i: (b * nt + i, cb + c)),
                  pl.BlockSpec((None, 3, wc), lambda b, c, i: (b, 0, c)),
                  pl.BlockSpec((None, 4, wc), lambda b, c, i: (layer, 0, c)),
                  pl.BlockSpec((None, 1, wc), lambda b, c, i: (layer, 0, c))],
        out_specs=[pl.BlockSpec((ts, wc), lambda b, c, i: (b * nt + i, c)),
                   pl.BlockSpec((None, 3, wc), lambda b, c, i: (b, 0, c))],
        out_shape=[jax.ShapeDtypeStruct((n_seq * seq_len, cm), F32),
                   jax.ShapeDtypeStruct((n_seq, 3, cm), F32)],
        scratch_shapes=[pltpu.VMEM((SUBLANES, wc), F32)],
        compiler_params=_cp("parallel", "parallel", "arbitrary"), name="conv_m",
    )(proj, buf, conv_w, conv_b.reshape(depth, 1, cm))


def _conv_a_step_kernel(ab_ref, ac_ref, ah_ref, b0_ref, b1_ref, w_ref, ga_ref, n0_ref, n1_ref):
    u = ac_ref[...] * ah_ref[...]
    w = w_ref[...]
    conv = w[0:1] * b0_ref[...] + w[1:2] * b1_ref[...] + w[2:3] * u
    ga_ref[...] = (ab_ref[...] * conv).astype(ga_ref.dtype)
    n0_ref[...] = b1_ref[...]
    n1_ref[...] = u


def _conv_a_step(proj, buf2, conv_w, layer, col0, ca):
    n = proj.shape[0]
    wc = _tile(ca, 2048)
    ncol = ca // wc
    cb = col0 // wc

    def col_spec(k):
        return pl.BlockSpec((n, wc), lambda c: (0, cb + k * ncol + c))

    def buf_spec(k):
        return pl.BlockSpec((None, n, wc), lambda c: (layer, 0, k * ncol + c))

    ga, n0, n1 = pl.pallas_call(
        _conv_a_step_kernel, grid=(ncol,),
        in_specs=[col_spec(0), col_spec(1), col_spec(2), buf_spec(0), buf_spec(1),
                  pl.BlockSpec((None, 3, wc), lambda c: (layer, 0, c))],
        out_specs=[pl.BlockSpec((n, wc), lambda c: (0, c))] * 3,
        out_shape=[jax.ShapeDtypeStruct((n, ca), BF16), jax.ShapeDtypeStruct((n, ca), F32),
                   jax.ShapeDtypeStruct((n, ca), F32)],
        compiler_params=_cp("parallel"), name="conv_a_step",
    )(proj, proj, proj, buf2, buf2, conv_w)
    return ga, jnp.stack([n0, n1], axis=1)


def _conv_m_step_kernel(x_ref, b0_ref, b1_ref, b2_ref, w_ref, b_ref, o_ref, n0_ref, n1_ref, n2_ref):
    x = x_ref[...]
    w = w_ref[...]
    conv = w[0:1] * b0_ref[...] + w[1:2] * b1_ref[...] + w[2:3] * b2_ref[...] + w[3:4] * x + b_ref[...]
    o_ref[...] = jax.nn.silu(conv)
    n0_ref[...] = b1_ref[...]
    n1_ref[...] = b2_ref[...]
    n2_ref[...] = x


def _conv_m_step(proj, buf3, conv_w, conv_b, layer, col0, cm):
    n = proj.shape[0]
    wc = _tile(math.gcd(col0, cm), 2048)
    ncol = cm // wc
    cb = col0 // wc
    depth = conv_w.shape[0]

    def buf_spec(k):
        return pl.BlockSpec((None, n, wc), lambda c: (layer, 0, k * ncol + c))

    act, n0, n1, n2 = pl.pallas_call(
        _conv_m_step_kernel, grid=(ncol,),
        in_specs=[pl.BlockSpec((n, wc), lambda c: (0, cb + c)), buf_spec(0), buf_spec(1), buf_spec(2),
                  pl.BlockSpec((None, 4, wc), lambda c: (layer, 0, c)),
                  pl.BlockSpec((None, 1, wc), lambda c: (layer, 0, c))],
        out_specs=[pl.BlockSpec((n, wc), lambda c: (0, c))] * 4,
        out_shape=[jax.ShapeDtypeStruct((n, cm), F32)] * 4,
        compiler_params=_cp("parallel"), name="conv_m_step",
    )(proj, buf3, buf3, buf3, conv_w, conv_b.reshape(depth, 1, cm))
    return act, jnp.stack([n0, n1, n2], axis=1)


def _softplus(x):
    return jnp.maximum(x, 0.0) + jnp.log1p(jnp.exp(-jnp.abs(x)))


def _gated_group_norm(y, z, gn, n_groups):
    yz = y * jax.nn.silu(z)
    gw = y.shape[1] // n_groups
    outs = []
    for g in range(n_groups):
        blk = yz[:, g * gw:(g + 1) * gw]
        ms = jnp.mean(blk * blk, axis=-1, keepdims=True)
        outs.append(blk * lax.rsqrt(ms + NORM_EPS) * gn[:, g * gw:(g + 1) * gw])
    return jnp.concatenate(outs, axis=1)


def _expand_heads(x, rexp3):
    hi = x.astype(BF16)
    r1 = x - hi.astype(F32)
    mid = r1.astype(BF16)
    lo = (r1 - mid.astype(F32)).astype(BF16)
    return jnp.dot(jnp.concatenate([hi, mid, lo], axis=1), rexp3, preferred_element_type=F32)


def _ssd_kernel(n_heads, hd, ns, n_groups, nz, *refs):
    xs_ref, bm_ref, cm_ref = refs[0:3]
    z_refs = refs[3:3 + nz]
    dt_ref, dtb_ref, alog_ref, drep_ref, gn_ref, rexp3_ref, y_ref, st_ref, s_ref = refs[3 + nz:]
    q = xs_ref.shape[0]
    r = n_heads // n_groups
    gw = r * hd

    @pl.when(pl.program_id(1) == 0)
    def _():
        s_ref[...] = jnp.zeros_like(s_ref)

    lane = lax.broadcasted_iota(jnp.int32, (q, LANES), 1)
    dt = jnp.where(lane < n_heads, _softplus(dt_ref[...] + dtb_ref[...]), 0.0)
    da = dt * (-jnp.exp(alog_ref[...]))
    rowi = lax.broadcasted_iota(jnp.int32, (q, q), 0)
    coli = lax.broadcasted_iota(jnp.int32, (q, q), 1)
    causal = rowi >= coli
    tri = causal.astype(F32)
    acs = jnp.dot(tri, da, precision=HI, preferred_element_type=F32)
    acs_t = acs.T
    rexp3 = rexp3_ref[...]
    dt_rep = _expand_heads(dt, rexp3)
    acs_rep = _expand_heads(acs, rexp3)
    xs = xs_ref[...]
    xdt = xs * dt_rep
    last = acs_rep[q - 1:q, :]
    xw = (xdt * jnp.exp(last - acs_rep)).astype(BF16)
    xdt_b = xdt.astype(BF16)
    eacs = jnp.exp(acs_rep)
    lane_q = lax.broadcasted_iota(jnp.int32, (q, LANES), 1)
    heads_per_blk = LANES // hd
    y_groups = []
    for g in range(n_groups):
        bg = bm_ref[:, g * ns:(g + 1) * ns].astype(BF16)
        cg = cm_ref[:, g * ns:(g + 1) * ns].astype(BF16)
        cb = lax.dot_general(cg, bg, NT_DIMS, preferred_element_type=F32)
        sg = s_ref[g * gw:(g + 1) * gw, :]
        y_off = lax.dot_general(cg, sg.astype(BF16), NT_DIMS, preferred_element_type=F32)
        blks = []
        for jb in range(gw // LANES):
            off = g * gw + jb * LANES
            xblk = xdt_b[:, off:off + LANES]
            yd = jnp.zeros((q, LANES), F32)
            for hh in range(heads_per_blk):
                h = off // hd + hh
                seg = acs[:, h:h + 1] - acs_t[h:h + 1, :]
                m = (cb * jnp.where(causal, jnp.exp(seg), 0.0)).astype(BF16)
                xm = jnp.where((lane_q >= hh * hd) & (lane_q < (hh + 1) * hd), xblk, jnp.zeros_like(xblk))
                yd = yd + jnp.dot(m, xm, preferred_element_type=F32)
            blks.append(yd)
        y_diag = jnp.concatenate(blks, axis=1)
        cols = slice(g * gw, (g + 1) * gw)
        y_groups.append(y_diag + y_off * eacs[:, cols] + drep_ref[:, cols] * xs[:, cols])
        contrib = lax.dot_general(xw[:, cols], bg, TN_DIMS, preferred_element_type=F32)
        decay_col = jnp.exp(jnp.broadcast_to(last[:, cols], (LANES, gw)).T)
        s_ref[g * gw:(g + 1) * gw, :] = decay_col[:, :ns] * sg + contrib
    y = jnp.concatenate(y_groups, axis=1)
    z = jnp.concatenate([zr[...] for zr in z_refs], axis=1)
    y_ref[...] = _gated_group_norm(y, z, gn_ref[...], n_groups).astype(y_ref.dtype)
    st_ref[...] = s_ref[...]


def _ssd_consts(dt_bias, a_log, d_skip, hd):
    depth, n_heads = dt_bias.shape
    pad = lambda t: jnp.pad(t, ((0, 0), (0, LANES - n_heads))).reshape(depth, 1, LANES)
    rep = lambda t: jnp.repeat(t, hd, axis=1).reshape(depth, 1, n_heads * hd)
    rexp = jnp.repeat(jnp.eye(LANES, n_heads, dtype=BF16), hd, axis=1)
    return pad(dt_bias), pad(a_log), rep(a_log), rep(d_skip), jnp.concatenate([rexp] * 3, axis=0)


def _ssd_prompt(act, proj, consts, gn, layer, n_seq, seq_len, dims, z_col0, dt_col0, zw):
    n_heads, hd, ns, n_groups = dims
    di, gn_w = n_heads * hd, n_groups * ns
    q = min(SSD_CHUNK, seq_len)
    assert seq_len % q == 0 and n_heads <= LANES and ns == LANES and LANES % hd == 0
    assert di % gn_w == 0 and z_col0 % zw == 0 and di % zw == 0 and dt_col0 % LANES == 0
    nc = seq_len // q
    nz = di // zw
    dtb, alog, _, d_rep, rexp3 = consts
    depth = gn.shape[0]
    row = lambda b, c: b * nc + c
    vec = lambda w: pl.BlockSpec((None, 1, w), lambda b, c: (layer, 0, 0))
    z_specs = [pl.BlockSpec((q, zw), lambda b, c, k=k: (row(b, c), z_col0 // zw + k)) for k in range(nz)]
    return pl.pallas_call(
        functools.partial(_ssd_kernel, n_heads, hd, ns, n_groups, nz), grid=(n_seq, nc),
        in_specs=[pl.BlockSpec((q, di), lambda b, c: (row(b, c), 0)),
                  pl.BlockSpec((q, gn_w), lambda b, c: (row(b, c), di // gn_w)),
                  pl.BlockSpec((q, gn_w), lambda b, c: (row(b, c), di // gn_w + 1))]
        + z_specs
        + [pl.BlockSpec((q, LANES), lambda b, c: (row(b, c), dt_col0 // LANES)),
           vec(LANES), vec(LANES), vec(di), vec(di),
           pl.BlockSpec((3 * LANES, di), lambda b, c: (0, 0))],
        out_specs=[pl.BlockSpec((q, di), lambda b, c: (row(b, c), 0)),
                   pl.BlockSpec((None, di, ns), lambda b, c: (b, 0, 0))],
        out_shape=[jax.ShapeDtypeStruct((n_seq * seq_len, di), BF16),
                   jax.ShapeDtypeStruct((n_seq, di, ns), F32)],
        scratch_shapes=[pltpu.VMEM((di, ns), F32)],
        compiler_params=_cp("parallel", "arbitrary"), name="ssd",
    )(act, act, act, *([proj] * nz), proj, dtb, alog, d_rep, gn.reshape(depth, 1, di), rexp3)


def _ssd_step_kernel(n_heads, hd, ns, n_groups, nz, *refs):
    xs_ref, bm_ref, cm_ref = refs[0:3]
    z_refs = refs[3:3 + nz]
    (dt_ref, dtb_ref, alogrep_ref, drep_ref, gn_ref, rexp3_ref, sin_ref,
     y_ref, sout_ref, dtx_t, dec_t, y_t) = refs[3 + nz:]
    b = pl.program_id(0)
    nb = xs_ref.shape[0]
    gw = (n_heads // n_groups) * hd

    @pl.when(b == 0)
    def _():
        lane = lax.broadcasted_iota(jnp.int32, (nb, LANES), 1)
        dt = jnp.where(lane < n_heads, _softplus(dt_ref[...] + dtb_ref[...]), 0.0)
        dt_rep = _expand_heads(dt, rexp3_ref[...])
        dtx_t[...] = (xs_ref[...] * dt_rep).T
        dec_t[...] = jnp.exp(dt_rep * (-jnp.exp(alogrep_ref[...]))).T
        y_t[...] = jnp.zeros_like(y_t)

    sel = lax.broadcasted_iota(jnp.int32, (gw, nb), 1) == b
    bm_row = bm_ref[pl.ds(b, 1), :]
    cm_row = cm_ref[pl.ds(b, 1), :]
    for g in range(n_groups):
        rows = slice(g * gw, (g + 1) * gw)
        xcol = jnp.sum(jnp.where(sel, dtx_t[rows, :], 0.0), axis=1, keepdims=True)
        dcol = jnp.sum(jnp.where(sel, dec_t[rows, :], 0.0), axis=1, keepdims=True)
        brow = bm_row[:, g * ns:(g + 1) * ns]
        crow = cm_row[:, g * ns:(g + 1) * ns]
        s_new = dcol * sin_ref[rows, :] + xcol * brow
        sout_ref[rows, :] = s_new
        cmat = jnp.broadcast_to(crow, (nb, ns)).astype(BF16)
        y_all = lax.dot_general(s_new.astype(BF16), cmat, NT_DIMS, preferred_element_type=F32)
        y_t[rows, :] = jnp.where(sel, y_all, y_t[rows, :])

    @pl.when(b == nb - 1)
    def _():
        y = y_t[...].T + drep_ref[...] * xs_ref[...]
        z = jnp.concatenate([zr[...] for zr in z_refs], axis=1)
        y_ref[...] = _gated_group_norm(y, z, gn_ref[...], n_groups).astype(y_ref.dtype)


def _ssd_step(act, proj, consts, gn, state, layer, dims, z_col0, dt_col0, zw):
    n_heads, hd, ns, n_groups = dims
    di, gn_w = n_heads * hd, n_groups * ns
    n = act.shape[0]
    assert n % LANES == 0 and ns == LANES
    nz = di // zw
    dtb, _, alog_rep, d_rep, rexp3 = consts
    depth = gn.shape[0]
    vec = lambda w: pl.BlockSpec((None, 1, w), lambda b: (layer, 0, 0))
    state_spec = pl.BlockSpec((None, None, di, ns), lambda b: (layer, b, 0, 0))
    n_in = 3 + nz + 6
    z_specs = [pl.BlockSpec((n, zw), lambda b, k=k: (0, z_col0 // zw + k)) for k in range(nz)]
    return pl.pallas_call(
        functools.partial(_ssd_step_kernel, n_heads, hd, ns, n_groups, nz), grid=(n,),
        in_specs=[pl.BlockSpec((n, di), lambda b: (0, 0)),
                  pl.BlockSpec((n, gn_w), lambda b: (0, di // gn_w)),
                  pl.BlockSpec((n, gn_w), lambda b: (0, di // gn_w + 1))]
        + z_specs
        + [pl.BlockSpec((n, LANES), lambda b: (0, dt_col0 // LANES)),
           vec(LANES), vec(di), vec(di), vec(di),
           pl.BlockSpec((3 * LANES, di), lambda b: (0, 0)),
           state_spec],
        out_specs=[pl.BlockSpec((n, di), lambda b: (0, 0)), state_spec],
        out_shape=[jax.ShapeDtypeStruct((n, di), BF16),
                   jax.ShapeDtypeStruct(state.shape, F32)],
        input_output_aliases={n_in: 1},
        scratch_shapes=[pltpu.VMEM((di, n), F32)] * 3,
        compiler_params=_cp("arbitrary"), name="ssd_step",
    )(act, act, act, *([proj] * nz), proj, dtb, alog_rep, d_rep, gn.reshape(depth, 1, di), rexp3, state)


def _peer_pairs():
    k1 = PEER_TOPK + 1
    return [(i, j) for i in range(k1) for j in range(k1) if (i + 1) * (j + 1) <= k1]


def _top_rows(x, count):
    vals = []
    for it in range(count):
        m = jnp.max(x, axis=0, keepdims=True)
        vals.append(m)
        if it + 1 < count:
            x = jnp.where(x == m, -jnp.inf, x)
    return vals


def _peer_select_kernel(n_heads, q_ref, k_ref, a1_ref, cq_ref, a2_ref, cand_ref):
    nk = k_ref.shape[2]
    pairs = _peer_pairs()
    npad = cand_ref.shape[0]
    cand_ref[len(pairs):npad, :] = jnp.full((npad - len(pairs), cand_ref.shape[1]), -jnp.inf, F32)
    for h in range(n_heads):
        s = []
        for half in range(2):
            qh = q_ref[:, (2 * h + half) * nk:(2 * h + half + 1) * nk].astype(BF16)
            s.append(lax.dot_general(k_ref[h, half].astype(BF16), qh, NT_DIMS, preferred_element_type=F32))
        top1 = _top_rows(s[0], PEER_TOPK + 1)
        top2 = _top_rows(s[1], PEER_TOPK + 1)
        for idx, (i, j) in enumerate(pairs):
            cand_ref[idx:idx + 1, :] = top1[i] + top2[j]
        cand = cand_ref[...]
        best = _top_rows(cand, PEER_TOPK + 1)
        cut = 0.5 * (best[PEER_TOPK - 1] + best[PEER_TOPK])
        c0 = top1[0] + top2[0]
        zsum = jnp.sum(jnp.where(cand > cut, jnp.exp(cand - c0), 0.0), axis=0, keepdims=True)
        a1_ref[h] = jnp.exp(s[0] - top1[0]) / zsum
        a2_ref[h] = jnp.exp(s[1] - top2[0])
        cq_ref[h] = jnp.exp((cut - top2[0]) - s[0])


def _peer_select(qv, keys, layer, tm):
    t = qv.shape[0]
    _, n_heads, _, nk, dh = keys.shape
    assert nk == LANES and dh == LANES
    npad = -(-len(_peer_pairs()) // SUBLANES) * SUBLANES
    out = jax.ShapeDtypeStruct((n_heads, nk, t), F32)
    ospec = pl.BlockSpec((n_heads, nk, tm), lambda i: (0, 0, i))
    return pl.pallas_call(
        functools.partial(_peer_select_kernel, n_heads), grid=(t // tm,),
        in_specs=[pl.BlockSpec((tm, qv.shape[1]), lambda i: (i, 0)),
                  pl.BlockSpec((None, n_heads, 2, nk, dh), lambda i: (layer, 0, 0, 0, 0))],
        out_specs=[ospec, ospec, ospec], out_shape=[out, out, out],
        scratch_shapes=[pltpu.VMEM((npad, tm), F32)],
        compiler_params=_cp("parallel"), name="peer_select",
    )(qv, keys)


def _gelu(x):
    return 0.5 * x * (1.0 + lax.erf(x * math.sqrt(0.5)))


def _peer_main_kernel(n_heads, nk, h_ref, u_ref, v_ref, a1_ref, cq_ref, a2_ref, o_ref):
    e = pl.program_id(1)
    te = u_ref.shape[0]

    @pl.when(e == 0)
    def _():
        o_ref[...] = jnp.zeros_like(o_ref)

    st = lax.dot_general(u_ref[...], h_ref[...], NT_DIMS, preferred_element_type=F32)
    parts = []
    for k in range(te // nk):
        e1 = e * (te // nk) + k
        w = None
        for h in range(n_heads):
            a2 = a2_ref[h]
            term = jnp.where(a2 >= cq_ref[h, pl.ds(e1, 1), :], a2, 0.0) * a1_ref[h, pl.ds(e1, 1), :]
            w = term if w is None else w + term
        parts.append((w * _gelu(st[k * nk:(k + 1) * nk, :])).astype(BF16))
    pt = jnp.concatenate(parts, axis=0)
    o_ref[...] += lax.dot_general(pt, v_ref[...], TN_DIMS, preferred_element_type=F32)


def _peer_main(h2, u_b, v_b, a1, cq, a2, layer, tm, te):
    t, d = h2.shape
    n_exp = u_b.shape[1]
    n_heads, nk, _ = a1.shape
    once = pl.Buffered(1)
    fac = pl.BlockSpec((n_heads, nk, tm), lambda i, e: (0, 0, i), pipeline_mode=once)
    return pl.pallas_call(
        functools.partial(_peer_main_kernel, n_heads, nk), grid=(t // tm, n_exp // te),
        in_specs=[pl.BlockSpec((tm, d), lambda i, e: (i, 0), pipeline_mode=once),
                  pl.BlockSpec((None, te, d), lambda i, e: (layer, e, 0)),
                  pl.BlockSpec((None, te, d), lambda i, e: (layer, e, 0)),
                  fac, fac, fac],
        out_specs=pl.BlockSpec((tm, d), lambda i, e: (i, 0)),
        out_shape=jax.ShapeDtypeStruct((t, d), F32),
        compiler_params=_cp("parallel", "arbitrary"), name="peer_main",
    )(h2, u_b, v_b, a1, cq, a2)


def kernel(x_prompt, x_sample, c_prompt, c_sample, state_conv_a, state_conv_m, state_ssm, w_ada, b_ada, norm1_g, w_in, conv_a_w, w_out_a, conv_m_w, conv_m_b, dt_bias, a_log, d_skip, ssm_norm_g, w_out_m, w_o, norm2_g, peer_wq, peer_keys, peer_u, peer_v, final_g):
    bp, seq, d = x_prompt.shape
    bs = x_sample.shape[0]
    depth = w_ada.shape[0]
    ca, cm = conv_a_w.shape[2], conv_m_w.shape[2]
    n_heads, hd, ns = dt_bias.shape[1], state_ssm.shape[3], state_ssm.shape[4]
    di = n_heads * hd
    n_groups = (cm - di) // (2 * ns)
    dims = (n_heads, hd, ns, n_groups)
    col_b = 2 * d
    col_z = col_b + 3 * ca
    col_x = col_z + di
    col_dt = col_x + cm
    zw = _tile(math.gcd(col_z, di), 2048)

    mod = _ada(jnp.concatenate([c_prompt, c_sample], axis=0), w_ada, b_ada).reshape(depth, bp + bs, N_MOD, d)
    grp_p = _Group(bp, seq, mod[:, :bp].reshape(depth, bp, N_MOD, 1, d))
    grp_s = _Group(bs, 1, jnp.transpose(mod[:, bp:], (0, 2, 1, 3)))

    consts = _ssd_consts(dt_bias, a_log, d_skip, hd)
    norm1 = norm1_g.reshape(depth, 1, d)
    norm2 = norm2_g.reshape(depth, 1, d)
    final = final_g.reshape(1, 1, d)
    u_b = peer_u.astype(BF16)
    v_b = peer_v.astype(BF16)
    st_a = state_conv_a.reshape(depth, bs, 2 * ca)
    st_m = state_conv_m.reshape(depth, bs, 3 * cm)
    st_h = state_ssm.reshape(depth, bs, di, ns)
    zero_a = jnp.zeros((bp, 2, ca), F32)
    zero_m = jnp.zeros((bp, 3, cm), F32)

    def layer_fn(grp, l, x, peer_prev, ssm_state):
        t = grp.tokens
        tm = grp.row_tile(1024)
        if peer_prev is None:
            h = _norm(grp, l, x, norm1, l, mod_k=(1, 0))
        else:
            x, h = _norm(grp, l, x, norm1, l, add=peer_prev, gate_k=5, gate_layer=l - 1, mod_k=(1, 0))
        proj = _mm(h, w_in, l, tm, 512, "in_proj")
        if grp.per_token:
            ga, new_a = _conv_a_step(proj, st_a, conv_a_w, l, col_b, ca)
            act, new_m = _conv_m_step(proj, st_m, conv_m_w, conv_m_b, l, col_x, cm)
            gm, new_h = _ssd_step(act, proj, consts, ssm_norm_g, ssm_state, l, dims, col_z, col_dt, zw)
        else:
            ga, new_a = _conv_a_prompt(proj, zero_a, conv_a_w, l, grp.n_seq, grp.seq_len, col_b, ca)
            act, new_m = _conv_m_prompt(proj, zero_m, conv_m_w, conv_m_b, l, grp.n_seq, grp.seq_len, col_x, cm)
            gm, new_h = _ssd_prompt(act, proj, consts, ssm_norm_g, l, grp.n_seq, grp.seq_len, dims,
                                    col_z, col_dt, zw)
        mix = _merge(ga, gm, proj, w_out_a, w_out_m, l, d, tm, _tile(d, 256))
        x = _resid_mm(grp, l, mix, w_o, x, 2, tm, _tile(d, 512))
        h2 = _norm(grp, l, x, norm2, l, mod_k=(4, 3))
        qv = _mm(h2, peer_wq, l, tm, 512, "peer_q")
        a1, cq, a2 = _peer_select(qv, peer_keys, l, _tile(t, 256))
        peer = _peer_main(h2, u_b, v_b, a1, cq, a2, l, _tile(t, 512), 512)
        return x, peer, new_a, new_m, new_h

    xp, xs = x_prompt.reshape(bp * seq, d), x_sample.reshape(bs, d)
    pp = ps = None
    outs = [[] for _ in range(5)]
    for l in range(depth):
        xp, pp, pa, pm, ph = layer_fn(grp_p, l, xp, pp, None)
        xs, ps, sa, sm, st_h = layer_fn(grp_s, l, xs, ps, st_h)
        for lst, val in zip(outs, (pa, pm, ph.reshape(bp, n_heads, hd, ns), sa, sm)):
            lst.append(val)
    y_p = _norm(grp_p, depth - 1, xp, final, 0, add=pp, gate_k=5, gate_layer=depth - 1)
    y_s = _norm(grp_s, depth - 1, xs, final, 0, add=ps, gate_k=5, gate_layer=depth - 1)
    stacked = [jnp.stack(o) for o in outs]
    return (y_p.reshape(bp, seq, d), y_s.reshape(bs, 1, d), *stacked, st_h.reshape(state_ssm.shape))
```

```python
import functools
import math

import jax
import jax.numpy as jnp
from jax import lax
from jax.experimental import pallas as pl
from jax.experimental.pallas import tpu as pltpu

F32 = jnp.float32
BF16 = jnp.bfloat16
NORM_EPS = 1e-6
PEER_TOPK = 16
SSD_CHUNK = 128
N_MOD = 6
LANES = 128
SUBLANES = 8
VMEM_LIMIT = 56 * 1024 * 1024
HI = lax.Precision.HIGHEST
NT_DIMS = (((1,), (1,)), ((), ()))
TN_DIMS = (((0,), (0,)), ((), ()))


def _cp(*sem):
    return pltpu.CompilerParams(dimension_semantics=sem, vmem_limit_bytes=VMEM_LIMIT)


def _tile(n, pref, mult=LANES):
    if n <= pref:
        return n
    t = (pref // mult) * mult
    while t >= mult:
        if n % t == 0:
            return t
        t -= mult
    return n


class _Group:
    def __init__(self, n_seq, seq_len, mod):
        self.n_seq, self.seq_len, self.mod = n_seq, seq_len, mod
        self.tokens = n_seq * seq_len
        self.per_token = seq_len == 1

    def row_tile(self, pref):
        if self.per_token:
            return _tile(self.tokens, pref, SUBLANES)
        return _tile(self.seq_len, pref, SUBLANES)

    def mod_spec(self, layer, k, tm, tn=None, col=False):
        d = self.mod.shape[-1]
        tn_ = d if tn is None else tn
        if self.per_token:
            if col:
                return pl.BlockSpec((None, None, tm, tn_), lambda i, j, *_: (layer, k, i, j))
            return pl.BlockSpec((None, None, tm, tn_), lambda i, *_: (layer, k, i, 0))
        sl = self.seq_len
        if col:
            return pl.BlockSpec((None, None, None, 1, tn_), lambda i, j, *_: (layer, (i * tm) // sl, k, 0, j))
        return pl.BlockSpec((None, None, None, 1, tn_), lambda i, *_: (layer, (i * tm) // sl, k, 0, 0))


def _ada_kernel(c_ref, w_ref, b_ref, o_ref):
    sc = jax.nn.silu(c_ref[...]).astype(BF16)
    o_ref[...] = jnp.dot(sc, w_ref[...].astype(BF16), preferred_element_type=F32) + b_ref[...]


def _ada(c_all, w_ada, b_ada):
    depth, d, n = w_ada.shape
    bc = c_all.shape[0]
    tn = _tile(n, 1024)
    return pl.pallas_call(
        _ada_kernel, grid=(depth, n // tn),
        in_specs=[pl.BlockSpec((bc, d), lambda l, j: (0, 0)),
                  pl.BlockSpec((None, d, tn), lambda l, j: (l, 0, j)),
                  pl.BlockSpec((None, 1, tn), lambda l, j: (l, 0, j))],
        out_specs=pl.BlockSpec((None, bc, tn), lambda l, j: (l, 0, j)),
        out_shape=jax.ShapeDtypeStruct((depth, bc, n), F32),
        compiler_params=_cp("parallel", "parallel"), name="ada",
    )(c_all, w_ada, b_ada.reshape(depth, 1, n))


def _norm_kernel(has_add, has_mod, *refs):
    refs = list(refs)
    x_ref = refs.pop(0)
    x = x_ref[...]
    if has_add:
        add_ref, gate_ref = refs.pop(0), refs.pop(0)
        x = x + gate_ref[...] * add_ref[...]
    g_ref = refs.pop(0)
    if has_mod:
        sc_ref, sh_ref = refs.pop(0), refs.pop(0)
    if has_add and has_mod:
        xo_ref = refs.pop(0)
        xo_ref[...] = x
    h_ref = refs.pop(0)
    xn = x * lax.rsqrt(jnp.mean(x * x, axis=-1, keepdims=True) + NORM_EPS) * g_ref[...]
    if has_mod:
        xn = xn * (1.0 + sc_ref[...]) + sh_ref[...]
    h_ref[...] = xn.astype(h_ref.dtype)


def _norm(grp, layer, x, g, g_layer, add=None, gate_k=None, gate_layer=None, mod_k=None):
    t, d = x.shape
    tm = grp.row_tile(256)
    has_add, has_mod = add is not None, mod_k is not None
    row = pl.BlockSpec((tm, d), lambda i: (i, 0))
    ins, specs = [x], [row]
    if has_add:
        ins += [add, grp.mod]
        specs += [row, grp.mod_spec(gate_layer, gate_k, tm)]
    ins.append(g)
    specs.append(pl.BlockSpec((None, 1, d), lambda i: (g_layer, 0, 0)))
    if has_mod:
        ins += [grp.mod, grp.mod]
        specs += [grp.mod_spec(layer, mod_k[0], tm), grp.mod_spec(layer, mod_k[1], tm)]
    out_shapes, out_specs = [], []
    if has_add and has_mod:
        out_shapes.append(jax.ShapeDtypeStruct((t, d), F32))
        out_specs.append(row)
    out_shapes.append(jax.ShapeDtypeStruct((t, d), BF16 if has_mod else F32))
    out_specs.append(row)
    res = pl.pallas_call(
        functools.partial(_norm_kernel, has_add, has_mod), grid=(t // tm,),
        in_specs=specs, out_specs=out_specs, out_shape=out_shapes,
        compiler_params=_cp("parallel"), name="norm",
    )(*ins)
    return res if len(res) > 1 else res[0]


def _mm_kernel(x_ref, w_ref, o_ref):
    o_ref[...] = jnp.dot(x_ref[...], w_ref[...].astype(BF16), preferred_element_type=F32).astype(o_ref.dtype)


def _mm(x, w, layer, tm, tn, name):
    m, k = x.shape
    n = w.shape[2]
    return pl.pallas_call(
        _mm_kernel, grid=(m // tm, pl.cdiv(n, tn)),
        in_specs=[pl.BlockSpec((tm, k), lambda i, j: (i, 0)),
                  pl.BlockSpec((None, k, tn), lambda i, j: (layer, 0, j))],
        out_specs=pl.BlockSpec((tm, tn), lambda i, j: (i, j)),
        out_shape=jax.ShapeDtypeStruct((m, n), F32),
        compiler_params=_cp("parallel", "parallel"), name=name,
    )(x, w)


def _mm_nt_kernel(x_ref, wt_ref, o_ref):
    o_ref[...] = lax.dot_general(x_ref[...], wt_ref[...].astype(BF16), NT_DIMS,
                                 preferred_element_type=F32).astype(o_ref.dtype)


def _mm_nt(x, wt, layer, tm, tn, name):
    m, k = x.shape
    n = wt.shape[1]
    return pl.pallas_call(
        _mm_nt_kernel, grid=(m // tm, pl.cdiv(n, tn)),
        in_specs=[pl.BlockSpec((tm, k), lambda i, j: (i, 0)),
                  pl.BlockSpec((None, tn, k), lambda i, j: (layer, j, 0))],
        out_specs=pl.BlockSpec((tm, tn), lambda i, j: (i, j)),
        out_shape=jax.ShapeDtypeStruct((m, n), F32),
        compiler_params=_cp("parallel", "parallel"), name=name,
    )(x, wt)


def _merge_kernel(ga_ref, wa_ref, gm_ref, wm_ref, sa_ref, sm_ref, o_ref):
    out_a = jnp.dot(ga_ref[...], wa_ref[...].astype(BF16), preferred_element_type=F32)
    out_m = jnp.dot(gm_ref[...], wm_ref[...].astype(BF16), preferred_element_type=F32)
    o_ref[...] = (jax.nn.sigmoid(sa_ref[...]) * out_a + jax.nn.sigmoid(sm_ref[...]) * out_m).astype(o_ref.dtype)


def _merge(ga, gm, proj, w_out_a, w_out_m, layer, d, tm, tn):
    m, ca = ga.shape
    di = gm.shape[1]
    nj = d // tn
    return pl.pallas_call(
        _merge_kernel, grid=(m // tm, nj),
        in_specs=[pl.BlockSpec((tm, ca), lambda i, j: (i, 0)),
                  pl.BlockSpec((None, ca, tn), lambda i, j: (layer, 0, j)),
                  pl.BlockSpec((tm, di), lambda i, j: (i, 0)),
                  pl.BlockSpec((None, di, tn), lambda i, j: (layer, 0, j)),
                  pl.BlockSpec((tm, tn), lambda i, j: (i, j)),
                  pl.BlockSpec((tm, tn), lambda i, j: (i, nj + j))],
        out_specs=pl.BlockSpec((tm, tn), lambda i, j: (i, j)),
        out_shape=jax.ShapeDtypeStruct((m, d), BF16),
        compiler_params=_cp("parallel", "parallel"), name="merge",
    )(ga, w_out_a, gm, w_out_m, proj, proj)


def _resid_kernel(mix_ref, w_ref, x_ref, gate_ref, o_ref):
    y = jnp.dot(mix_ref[...], w_ref[...].astype(BF16), preferred_element_type=F32)
    o_ref[...] = x_ref[...] + gate_ref[...] * y


def _resid_mm(grp, layer, mix, w_o, x, gate_k, tm, tn):
    m, k = mix.shape
    d = x.shape[1]
    return pl.pallas_call(
        _resid_kernel, grid=(m // tm, d // tn),
        in_specs=[pl.BlockSpec((tm, k), lambda i, j: (i, 0)),
                  pl.BlockSpec((None, k, tn), lambda i, j: (layer, 0, j)),
                  pl.BlockSpec((tm, tn), lambda i, j: (i, j)),
                  grp.mod_spec(layer, gate_k, tm, tn, col=True)],
        out_specs=pl.BlockSpec((tm, tn), lambda i, j: (i, j)),
        out_shape=jax.ShapeDtypeStruct((m, d), F32),
        compiler_params=_cp("parallel", "parallel"), name="resid_mm",
    )(mix, w_o, x, grp.mod)


CONV_HEAD_ROWS = 16


def _causal_conv(u, w):
    width = w.shape[0]
    y = w[width - 1:width] * u
    for s in range(1, width):
        y = y + w[width - 1 - s:width - s] * pltpu.roll(u, s, 0)
    return y


def _conv_head(u_head, carry, w):
    ext = jnp.concatenate([carry, u_head], axis=0)
    return _causal_conv(ext, w)[SUBLANES:, :]


def _conv_a_kernel(ts, ab_ref, ac_ref, ah_ref, buf_ref, w_ref, ga_ref, nb_ref, carry_ref):
    hr = CONV_HEAD_ROWS

    @pl.when(pl.program_id(2) == 0)
    def _():
        carry_ref[...] = jnp.zeros_like(carry_ref)
        carry_ref[SUBLANES - 2:SUBLANES, :] = buf_ref[...]

    w = w_ref[...]
    ga_ref[...] = (ab_ref[...] * _causal_conv(ac_ref[...] * ah_ref[...], w)).astype(ga_ref.dtype)
    head = _conv_head(ac_ref[0:hr, :] * ah_ref[0:hr, :], carry_ref[...], w)
    ga_ref[0:hr, :] = (ab_ref[0:hr, :] * head).astype(ga_ref.dtype)
    last = ac_ref[ts - 2:ts, :] * ah_ref[ts - 2:ts, :]
    carry_ref[SUBLANES - 2:SUBLANES, :] = last
    nb_ref[...] = last


def _conv_a_prompt(proj, buf, conv_w, layer, n_seq, seq_len, col0, ca):
    wc = _tile(ca, 2048)
    assert col0 % wc == 0
    ts = _tile(seq_len, 256, SUBLANES)
    nt = seq_len // ts
    ncol = ca // wc
    cb = col0 // wc

    def col_spec(k):
        return pl.BlockSpec((ts, wc), lambda b, c, i: (b * nt + i, cb + k * ncol + c))

    return pl.pallas_call(
        functools.partial(_conv_a_kernel, ts), grid=(n_seq, ncol, nt),
        in_specs=[col_spec(0), col_spec(1), col_spec(2),
                  pl.BlockSpec((None, 2, wc), lambda b, c, i: (b, 0, c)),
                  pl.BlockSpec((None, 3, wc), lambda b, c, i: (layer, 0, c))],
        out_specs=[pl.BlockSpec((ts, wc), lambda b, c, i: (b * nt + i, c)),
                   pl.BlockSpec((None, 2, wc), lambda b, c, i: (b, 0, c))],
        out_shape=[jax.ShapeDtypeStruct((n_seq * seq_len, ca), BF16),
                   jax.ShapeDtypeStruct((n_seq, 2, ca), F32)],
        scratch_shapes=[pltpu.VMEM((SUBLANES, wc), F32)],
        compiler_params=_cp("parallel", "parallel", "arbitrary"), name="conv_a",
    )(proj, proj, proj, buf, conv_w)


def _conv_m_kernel(ts, x_ref, buf_ref, w_ref, b_ref, o_ref, nb_ref, carry_ref):
    hr = CONV_HEAD_ROWS

    @pl.when(pl.program_id(2) == 0)
    def _():
        carry_ref[...] = jnp.zeros_like(carry_ref)
        carry_ref[SUBLANES - 3:SUBLANES, :] = buf_ref[...]

    w = w_ref[...]
    o_ref[...] = jax.nn.silu(_causal_conv(x_ref[...], w) + b_ref[...])
    o_ref[0:hr, :] = jax.nn.silu(_conv_head(x_ref[0:hr, :], carry_ref[...], w) + b_ref[...])
    last = x_ref[ts - 3:ts, :]
    carry_ref[SUBLANES - 3:SUBLANES, :] = last
    nb_ref[...] = last


def _conv_m_prompt(proj, buf, conv_w, conv_b, layer, n_seq, seq_len, col0, cm):
    wc = _tile(math.gcd(col0, cm), 2048)
    assert col0 % wc == 0 and cm % wc == 0
    ts = _tile(seq_len, 256, SUBLANES)
    nt = seq_len // ts
    cb = col0 // wc
    depth = conv_w.shape[0]
    return pl.pallas_call(
        functools.partial(_conv_m_kernel, ts), grid=(n_seq, cm // wc, nt),
        in_specs=[pl.BlockSpec((ts, wc), lambda b, c, i: (b * nt + i, cb + c)),
                  pl.BlockSpec((None, 3, wc), lambda b, c, i: (b, 0, c)),
                  pl.BlockSpec((None, 4, wc), lambda b, c, i: (layer, 0, c)),
                  pl.BlockSpec((None, 1, wc), lambda b, c, i: (layer, 0, c))],
        out_specs=[pl.BlockSpec((ts, wc), lambda b, c, i: (b * nt + i, c)),
                   pl.BlockSpec((None, 3, wc), lambda b, c, i: (b, 0, c))],
        out_shape=[jax.ShapeDtypeStruct((n_seq * seq_len, cm), F32),
                   jax.ShapeDtypeStruct((n_seq, 3, cm), F32)],
        scratch_shapes=[pltpu.VMEM((SUBLANES, wc), F32)],
        compiler_params=_cp("parallel", "parallel", "arbitrary"), name="conv_m",
    )(proj, buf, conv_w, conv_b.reshape(depth, 1, cm))


def _conv_a_step_kernel(ab_ref, ac_ref, ah_ref, b0_ref, b1_ref, w_ref, ga_ref, n0_ref, n1_ref):
    u = ac_ref[...] * ah_ref[...]
    w = w_ref[...]
    conv = w[0:1] * b0_ref[...] + w[1:2] * b1_ref[...] + w[2:3] * u
    ga_ref[...] = (ab_ref[...] * conv).astype(ga_ref.dtype)
    n0_ref[...] = b1_ref[...]
    n1_ref[...] = u


def _conv_a_step(proj, buf2, conv_w, layer, col0, ca):
    n = proj.shape[0]
    wc = _tile(ca, 2048)
    ncol = ca // wc
    cb = col0 // wc

    def col_spec(k):
        return pl.BlockSpec((n, wc), lambda c: (0, cb + k * ncol + c))

    def buf_spec(k):
        return pl.BlockSpec((None, n, wc), lambda c: (layer, 0, k * ncol + c))

    ga, n0, n1 = pl.pallas_call(
        _conv_a_step_kernel, grid=(ncol,),
        in_specs=[col_spec(0), col_spec(1), col_spec(2), buf_spec(0), buf_spec(1),
                  pl.BlockSpec((None, 3, wc), lambda c: (layer, 0, c))],
        out_specs=[pl.BlockSpec((n, wc), lambda c: (0, c))] * 3,
        out_shape=[jax.ShapeDtypeStruct((n, ca), BF16), jax.ShapeDtypeStruct((n, ca), F32),
                   jax.ShapeDtypeStruct((n, ca), F32)],
        compiler_params=_cp("parallel"), name="conv_a_step",
    )(proj, proj, proj, buf2, buf2, conv_w)
    return ga, jnp.stack([n0, n1], axis=1)


def _conv_m_step_kernel(x_ref, b0_ref, b1_ref, b2_ref, w_ref, b_ref, o_ref, n0_ref, n1_ref, n2_ref):
    x = x_ref[...]
    w = w_ref[...]
    conv = w[0:1] * b0_ref[...] + w[1:2] * b1_ref[...] + w[2:3] * b2_ref[...] + w[3:4] * x + b_ref[...]
    o_ref[...] = jax.nn.silu(conv)
    n0_ref[...] = b1_ref[...]
    n1_ref[...] = b2_ref[...]
    n2_ref[...] = x


def _conv_m_step(proj, buf3, conv_w, conv_b, layer, col0, cm):
    n = proj.shape[0]
    wc = _tile(math.gcd(col0, cm), 2048)
    ncol = cm // wc
    cb = col0 // wc
    depth = conv_w.shape[0]

    def buf_spec(k):
        return pl.BlockSpec((None, n, wc), lambda c: (layer, 0, k * ncol + c))

    act, n0, n1, n2 = pl.pallas_call(
        _conv_m_step_kernel, grid=(ncol,),
        in_specs=[pl.BlockSpec((n, wc), lambda c: (0, cb + c)), buf_spec(0), buf_spec(1), buf_spec(2),
                  pl.BlockSpec((None, 4, wc), lambda c: (layer, 0, c)),
                  pl.BlockSpec((None, 1, wc), lambda c: (layer, 0, c))],
        out_specs=[pl.BlockSpec((n, wc), lambda c: (0, c))] * 4,
        out_shape=[jax.ShapeDtypeStruct((n, cm), F32)] * 4,
        compiler_params=_cp("parallel"), name="conv_m_step",
    )(proj, buf3, buf3, buf3, conv_w, conv_b.reshape(depth, 1, cm))
    return act, jnp.stack([n0, n1, n2], axis=1)


def _softplus(x):
    return jnp.maximum(x, 0.0) + jnp.log1p(jnp.exp(-jnp.abs(x)))


def _gated_group_norm(y, z, gn, n_groups):
    yz = y * jax.nn.silu(z)
    gw = y.shape[1] // n_groups
    outs = []
    for g in range(n_groups):
        blk = yz[:, g * gw:(g + 1) * gw]
        ms = jnp.mean(blk * blk, axis=-1, keepdims=True)
        outs.append(blk * lax.rsqrt(ms + NORM_EPS) * gn[:, g * gw:(g + 1) * gw])
    return jnp.concatenate(outs, axis=1)


def _expand_heads(x, rexp3):
    hi = x.astype(BF16)
    r1 = x - hi.astype(F32)
    mid = r1.astype(BF16)
    lo = (r1 - mid.astype(F32)).astype(BF16)
    return jnp.dot(jnp.concatenate([hi, mid, lo], axis=1), rexp3, preferred_element_type=F32)


def _ssd_kernel(n_heads, hd, ns, n_groups, nz, *refs):
    xs_ref, bm_ref, cm_ref = refs[0:3]
    z_refs = refs[3:3 + nz]
    dt_ref, dtb_ref, alog_ref, drep_ref, gn_ref, rexp3_ref, y_ref, st_ref, s_ref = refs[3 + nz:]
    q = xs_ref.shape[0]
    r = n_heads // n_groups
    gw = r * hd

    @pl.when(pl.program_id(1) == 0)
    def _():
        s_ref[...] = jnp.zeros_like(s_ref)

    lane = lax.broadcasted_iota(jnp.int32, (q, LANES), 1)
    dt = jnp.where(lane < n_heads, _softplus(dt_ref[...] + dtb_ref[...]), 0.0)
    da = dt * (-jnp.exp(alog_ref[...]))
    rowi = lax.broadcasted_iota(jnp.int32, (q, q), 0)
    coli = lax.broadcasted_iota(jnp.int32, (q, q), 1)
    causal = rowi >= coli
    tri = causal.astype(F32)
    acs = jnp.dot(tri, da, precision=HI, preferred_element_type=F32)
    acs_t = acs.T
    rexp3 = rexp3_ref[...]
    dt_rep = _expand_heads(dt, rexp3)
    acs_rep = _expand_heads(acs, rexp3)
    xs = xs_ref[...]
    xdt = xs * dt_rep
    last = acs_rep[q - 1:q, :]
    xw = (xdt * jnp.exp(last - acs_rep)).astype(BF16)
    xdt_b = xdt.astype(BF16)
    eacs = jnp.exp(acs_rep)
    lane_q = lax.broadcasted_iota(jnp.int32, (q, LANES), 1)
    heads_per_blk = LANES // hd
    y_groups = []
    for g in range(n_groups):
        bg = bm_ref[:, g * ns:(g + 1) * ns].astype(BF16)
        cg = cm_ref[:, g * ns:(g + 1) * ns].astype(BF16)
        cb = lax.dot_general(cg, bg, NT_DIMS, preferred_element_type=F32)
        sg = s_ref[g * gw:(g + 1) * gw, :]
        y_off = lax.dot_general(cg, sg.astype(BF16), NT_DIMS, preferred_element_type=F32)
        blks = []
        for jb in range(gw // LANES):
            off = g * gw + jb * LANES
            xblk = xdt_b[:, off:off + LANES]
            yd = jnp.zeros((q, LANES), F32)
            for hh in range(heads_per_blk):
                h = off // hd + hh
                seg = acs[:, h:h + 1] - acs_t[h:h + 1, :]
                m = (cb * jnp.where(causal, jnp.exp(seg), 0.0)).astype(BF16)
                xm = jnp.where((lane_q >= hh * hd) & (lane_q < (hh + 1) * hd), xblk, jnp.zeros_like(xblk))
                yd = yd + jnp.dot(m, xm, preferred_element_type=F32)
            blks.append(yd)
        y_diag = jnp.concatenate(blks, axis=1)
        cols = slice(g * gw, (g + 1) * gw)
        y_groups.append(y_diag + y_off * eacs[:, cols] + drep_ref[:, cols] * xs[:, cols])
        contrib = lax.dot_general(xw[:, cols], bg, TN_DIMS, preferred_element_type=F32)
        decay_col = jnp.exp(jnp.broadcast_to(last[:, cols], (LANES, gw)).T)
        s_ref[g * gw:(g + 1) * gw, :] = decay_col[:, :ns] * sg + contrib
    y = jnp.concatenate(y_groups, axis=1)
    z = jnp.concatenate([zr[...] for zr in z_refs], axis=1)
    y_ref[...] = _gated_group_norm(y, z, gn_ref[...], n_groups).astype(y_ref.dtype)
    st_ref[...] = s_ref[...]


def _ssd_consts(dt_bias, a_log, d_skip, hd):
    depth, n_heads = dt_bias.shape
    pad = lambda t: jnp.pad(t, ((0, 0), (0, LANES - n_heads))).reshape(depth, 1, LANES)
    rep = lambda t: jnp.repeat(t, hd, axis=1).reshape(depth, 1, n_heads * hd)
    rexp = jnp.repeat(jnp.eye(LANES, n_heads, dtype=BF16), hd, axis=1)
    return pad(dt_bias), pad(a_log), rep(a_log), rep(d_skip), jnp.concatenate([rexp] * 3, axis=0)


def _ssd_prompt(act, proj, consts, gn, layer, n_seq, seq_len, dims, z_col0, dt_col0, zw):
    n_heads, hd, ns, n_groups = dims
    di, gn_w = n_heads * hd, n_groups * ns
    q = min(SSD_CHUNK, seq_len)
    assert seq_len % q == 0 and n_heads <= LANES and ns == LANES and LANES % hd == 0
    assert di % gn_w == 0 and z_col0 % zw == 0 and di % zw == 0 and dt_col0 % LANES == 0
    nc = seq_len // q
    nz = di // zw
    dtb, alog, _, d_rep, rexp3 = consts
    depth = gn.shape[0]
    row = lambda b, c: b * nc + c
    vec = lambda w: pl.BlockSpec((None, 1, w), lambda b, c: (layer, 0, 0))
    z_specs = [pl.BlockSpec((q, zw), lambda b, c, k=k: (row(b, c), z_col0 // zw + k)) for k in range(nz)]
    return pl.pallas_call(
        functools.partial(_ssd_kernel, n_heads, hd, ns, n_groups, nz), grid=(n_seq, nc),
        in_specs=[pl.BlockSpec((q, di), lambda b, c: (row(b, c), 0)),
                  pl.BlockSpec((q, gn_w), lambda b, c: (row(b, c), di // gn_w)),
                  pl.BlockSpec((q, gn_w), lambda b, c: (row(b, c), di // gn_w + 1))]
        + z_specs
        + [pl.BlockSpec((q, LANES), lambda b, c: (row(b, c), dt_col0 // LANES)),
           vec(LANES), vec(LANES), vec(di), vec(di),
           pl.BlockSpec((3 * LANES, di), lambda b, c: (0, 0))],
        out_specs=[pl.BlockSpec((q, di), lambda b, c: (row(b, c), 0)),
                   pl.BlockSpec((None, di, ns), lambda b, c: (b, 0, 0))],
        out_shape=[jax.ShapeDtypeStruct((n_seq * seq_len, di), BF16),
                   jax.ShapeDtypeStruct((n_seq, di, ns), F32)],
        scratch_shapes=[pltpu.VMEM((di, ns), F32)],
        compiler_params=_cp("parallel", "arbitrary"), name="ssd",
    )(act, act, act, *([proj] * nz), proj, dtb, alog, d_rep, gn.reshape(depth, 1, di), rexp3)


def _ssd_step_kernel(n_heads, hd, ns, n_groups, nz, *refs):
    xs_ref, bm_ref, cm_ref = refs[0:3]
    z_refs = refs[3:3 + nz]
    (dt_ref, dtb_ref, alogrep_ref, drep_ref, gn_ref, rexp3_ref, sin_ref,
     y_ref, sout_ref, dtx_t, dec_t, y_t) = refs[3 + nz:]
    b = pl.program_id(0)
    nb = xs_ref.shape[0]
    gw = (n_heads // n_groups) * hd

    @pl.when(b == 0)
    def _():
        lane = lax.broadcasted_iota(jnp.int32, (nb, LANES), 1)
        dt = jnp.where(lane < n_heads, _softplus(dt_ref[...] + dtb_ref[...]), 0.0)
        dt_rep = _expand_heads(dt, rexp3_ref[...])
        dtx_t[...] = (xs_ref[...] * dt_rep).T
        dec_t[...] = jnp.exp(dt_rep * (-jnp.exp(alogrep_ref[...]))).T
        y_t[...] = jnp.zeros_like(y_t)

    sel = lax.broadcasted_iota(jnp.int32, (gw, nb), 1) == b
    bm_row = bm_ref[pl.ds(b, 1), :]
    cm_row = cm_ref[pl.ds(b, 1), :]
    for g in range(n_groups):
        rows = slice(g * gw, (g + 1) * gw)
        xcol = jnp.sum(jnp.where(sel, dtx_t[rows, :], 0.0), axis=1, keepdims=True)
        dcol = jnp.sum(jnp.where(sel, dec_t[rows, :], 0.0), axis=1, keepdims=True)
        brow = bm_row[:, g * ns:(g + 1) * ns]
        crow = cm_row[:, g * ns:(g + 1) * ns]
        s_new = dcol * sin_ref[rows, :] + xcol * brow
        sout_ref[rows, :] = s_new
        cmat = jnp.broadcast_to(crow, (nb, ns)).astype(BF16)
        y_all = lax.dot_general(s_new.astype(BF16), cmat, NT_DIMS, preferred_element_type=F32)
        y_t[rows, :] = jnp.where(sel, y_all, y_t[rows, :])

    @pl.when(b == nb - 1)
    def _():
        y = y_t[...].T + drep_ref[...] * xs_ref[...]
        z = jnp.concatenate([zr[...] for zr in z_refs], axis=1)
        y_ref[...] = _gated_group_norm(y, z, gn_ref[...], n_groups).astype(y_ref.dtype)


def _ssd_step(act, proj, consts, gn, state, layer, dims, z_col0, dt_col0, zw):
    n_heads, hd, ns, n_groups = dims
    di, gn_w = n_heads * hd, n_groups * ns
    n = act.shape[0]
    assert n % LANES == 0 and ns == LANES
    nz = di // zw
    dtb, _, alog_rep, d_rep, rexp3 = consts
    depth = gn.shape[0]
    vec = lambda w: pl.BlockSpec((None, 1, w), lambda b: (layer, 0, 0))
    state_spec = pl.BlockSpec((None, None, di, ns), lambda b: (layer, b, 0, 0))
    n_in = 3 + nz + 6
    z_specs = [pl.BlockSpec((n, zw), lambda b, k=k: (0, z_col0 // zw + k)) for k in range(nz)]
    return pl.pallas_call(
        functools.partial(_ssd_step_kernel, n_heads, hd, ns, n_groups, nz), grid=(n,),
        in_specs=[pl.BlockSpec((n, di), lambda b: (0, 0)),
                  pl.BlockSpec((n, gn_w), lambda b: (0, di // gn_w)),
                  pl.BlockSpec((n, gn_w), lambda b: (0, di // gn_w + 1))]
        + z_specs
        + [pl.BlockSpec((n, LANES), lambda b: (0, dt_col0 // LANES)),
           vec(LANES), vec(di), vec(di), vec(di),
           pl.BlockSpec((3 * LANES, di), lambda b: (0, 0)),
           state_spec],
        out_specs=[pl.BlockSpec((n, di), lambda b: (0, 0)), state_spec],
        out_shape=[jax.ShapeDtypeStruct((n, di), BF16),
                   jax.ShapeDtypeStruct(state.shape, F32)],
        input_output_aliases={n_in: 1},
        scratch_shapes=[pltpu.VMEM((di, n), F32)] * 3,
        compiler_params=_cp("arbitrary"), name="ssd_step",
    )(act, act, act, *([proj] * nz), proj, dtb, alog_rep, d_rep, gn.reshape(depth, 1, di), rexp3, state)


def _peer_pairs():
    k1 = PEER_TOPK + 1
    return [(i, j) for i in range(k1) for j in range(k1) if (i + 1) * (j + 1) <= k1]


def _top_rows(x, count):
    vals = []
    for it in range(count):
        m = jnp.max(x, axis=0, keepdims=True)
        vals.append(m)
        if it + 1 < count:
            x = jnp.where(x == m, -jnp.inf, x)
    return vals


def _peer_select_kernel(n_heads, q_ref, k_ref, a1_ref, cq_ref, a2_ref, cand_ref):
    nk = k_ref.shape[2]
    pairs = _peer_pairs()
    npad = cand_ref.shape[0]
    cand_ref[len(pairs):npad, :] = jnp.full((npad - len(pairs), cand_ref.shape[1]), -jnp.inf, F32)
    for h in range(n_heads):
        s = []
        for half in range(2):
            qh = q_ref[:, (2 * h + half) * nk:(2 * h + half + 1) * nk].astype(BF16)
            s.append(lax.dot_general(k_ref[h, half].astype(BF16), qh, NT_DIMS, preferred_element_type=F32))
        top1 = _top_rows(s[0], PEER_TOPK + 1)
        top2 = _top_rows(s[1], PEER_TOPK + 1)
        for idx, (i, j) in enumerate(pairs):
            cand_ref[idx:idx + 1, :] = top1[i] + top2[j]
        cand = cand_ref[...]
        best = _top_rows(cand, PEER_TOPK + 1)
        cut = 0.5 * (best[PEER_TOPK - 1] + best[PEER_TOPK])
        c0 = top1[0] + top2[0]
        zsum = jnp.sum(jnp.where(cand > cut, jnp.exp(cand - c0), 0.0), axis=0, keepdims=True)
        a1_ref[h] = jnp.exp(s[0] - top1[0]) / zsum
        a2_ref[h] = jnp.exp(s[1] - top2[0])
        cq_ref[h] = jnp.exp((cut - top2[0]) - s[0])


def _peer_select(qv, keys, layer, tm):
    t = qv.shape[0]
    _, n_heads, _, nk, dh = keys.shape
    assert nk == LANES and dh == LANES
    npad = -(-len(_peer_pairs()) // SUBLANES) * SUBLANES
    out = jax.ShapeDtypeStruct((n_heads, nk, t), F32)
    ospec = pl.BlockSpec((n_heads, nk, tm), lambda i: (0, 0, i))
    return pl.pallas_call(
        functools.partial(_peer_select_kernel, n_heads), grid=(t // tm,),
        in_specs=[pl.BlockSpec((tm, qv.shape[1]), lambda i: (i, 0)),
                  pl.BlockSpec((None, n_heads, 2, nk, dh), lambda i: (layer, 0, 0, 0, 0))],
        out_specs=[ospec, ospec, ospec], out_shape=[out, out, out],
        scratch_shapes=[pltpu.VMEM((npad, tm), F32)],
        compiler_params=_cp("parallel"), name="peer_select",
    )(qv, keys)


def _gelu(x):
    return 0.5 * x * (1.0 + lax.erf(x * math.sqrt(0.5)))


def _peer_main_kernel(n_heads, nk, h_ref, u_ref, v_ref, a1_ref, cq_ref, a2_ref, o_ref):
    e = pl.program_id(1)
    te = u_ref.shape[0]

    @pl.when(e == 0)
    def _():
        o_ref[...] = jnp.zeros_like(o_ref)

    st = lax.dot_general(u_ref[...], h_ref[...], NT_DIMS, preferred_element_type=F32)
    parts = []
    for k in range(te // nk):
        e1 = e * (te // nk) + k
        w = None
        for h in range(n_heads):
            a2 = a2_ref[h]
            term = jnp.where(a2 >= cq_ref[h, pl.ds(e1, 1), :], a2, 0.0) * a1_ref[h, pl.ds(e1, 1), :]
            w = term if w is None else w + term
        parts.append((w * _gelu(st[k * nk:(k + 1) * nk, :])).astype(BF16))
    pt = jnp.concatenate(parts, axis=0)
    o_ref[...] += lax.dot_general(pt, v_ref[...], TN_DIMS, preferred_element_type=F32)


def _peer_main(h2, u_b, v_b, a1, cq, a2, layer, tm, te):
    t, d = h2.shape
    n_exp = u_b.shape[1]
    n_heads, nk, _ = a1.shape
    once = pl.Buffered(1)
    fac = pl.BlockSpec((n_heads, nk, tm), lambda i, e: (0, 0, i), pipeline_mode=once)
    return pl.pallas_call(
        functools.partial(_peer_main_kernel, n_heads, nk), grid=(t // tm, n_exp // te),
        in_specs=[pl.BlockSpec((tm, d), lambda i, e: (i, 0), pipeline_mode=once),
                  pl.BlockSpec((None, te, d), lambda i, e: (layer, e, 0)),
                  pl.BlockSpec((None, te, d), lambda i, e: (layer, e, 0)),
                  fac, fac, fac],
        out_specs=pl.BlockSpec((tm, d), lambda i, e: (i, 0)),
        out_shape=jax.ShapeDtypeStruct((t, d), F32),
        compiler_params=_cp("parallel", "arbitrary"), name="peer_main",
    )(h2, u_b, v_b, a1, cq, a2)


def kernel(x_prompt, x_sample, c_prompt, c_sample, state_conv_a, state_conv_m, state_ssm, w_ada, b_ada, norm1_g, w_in, conv_a_w, w_out_a, conv_m_w, conv_m_b, dt_bias, a_log, d_skip, ssm_norm_g, w_out_m, w_o, norm2_g, peer_wq, peer_keys, peer_u, peer_v, final_g):
    bp, seq, d = x_prompt.shape
    bs = x_sample.shape[0]
    depth = w_ada.shape[0]
    ca, cm = conv_a_w.shape[2], conv_m_w.shape[2]
    n_heads, hd, ns = dt_bias.shape[1], state_ssm.shape[3], state_ssm.shape[4]
    di = n_heads * hd
    n_groups = (cm - di) // (2 * ns)
    dims = (n_heads, hd, ns, n_groups)
    col_b = 2 * d
    col_z = col_b + 3 * ca
    col_x = col_z + di
    col_dt = col_x + cm
    zw = _tile(math.gcd(col_z, di), 2048)

    mod = _ada(jnp.concatenate([c_prompt, c_sample], axis=0), w_ada, b_ada).reshape(depth, bp + bs, N_MOD, d)
    grp_p = _Group(bp, seq, mod[:, :bp].reshape(depth, bp, N_MOD, 1, d))
    grp_s = _Group(bs, 1, jnp.transpose(mod[:, bp:], (0, 2, 1, 3)))

    consts = _ssd_consts(dt_bias, a_log, d_skip, hd)
    norm1 = norm1_g.reshape(depth, 1, d)
    norm2 = norm2_g.reshape(depth, 1, d)
    final = final_g.reshape(1, 1, d)
    w_in_t = jnp.swapaxes(w_in, 1, 2)
    u_b = peer_u.astype(BF16)
    v_b = peer_v.astype(BF16)
    st_a = state_conv_a.reshape(depth, bs, 2 * ca)
    st_m = state_conv_m.reshape(depth, bs, 3 * cm)
    st_h = state_ssm.reshape(depth, bs, di, ns)
    zero_a = jnp.zeros((bp, 2, ca), F32)
    zero_m = jnp.zeros((bp, 3, cm), F32)

    def layer_fn(grp, l, x, peer_prev, ssm_state):
        t = grp.tokens
        tm = grp.row_tile(1024)
        if peer_prev is None:
            h = _norm(grp, l, x, norm1, l, mod_k=(1, 0))
        else:
            x, h = _norm(grp, l, x, norm1, l, add=peer_prev, gate_k=5, gate_layer=l - 1, mod_k=(1, 0))
        proj = _mm_nt(h, w_in_t, l, tm, 512, "in_proj")
        if grp.per_token:
            ga, new_a = _conv_a_step(proj, st_a, conv_a_w, l, col_b, ca)
            act, new_m = _conv_m_step(proj, st_m, conv_m_w, conv_m_b, l, col_x, cm)
            gm, new_h = _ssd_step(act, proj, consts, ssm_norm_g, ssm_state, l, dims, col_z, col_dt, zw)
        else:
            ga, new_a = _conv_a_prompt(proj, zero_a, conv_a_w, l, grp.n_seq, grp.seq_len, col_b, ca)
            act, new_m = _conv_m_prompt(proj, zero_m, conv_m_w, conv_m_b, l, grp.n_seq, grp.seq_len, col_x, cm)
            gm, new_h = _ssd_prompt(act, proj, consts, ssm_norm_g, l, grp.n_seq, grp.seq_len, dims,
                                    col_z, col_dt, zw)
        mix = _merge(ga, gm, proj, w_out_a, w_out_m, l, d, tm, _tile(d, 256))
        x = _resid_mm(grp, l, mix, w_o, x, 2, tm, _tile(d, 512))
        h2 = _norm(grp, l, x, norm2, l, mod_k=(4, 3))
        qv = _mm(h2, peer_wq, l, tm, 512, "peer_q")
        a1, cq, a2 = _peer_select(qv, peer_keys, l, _tile(t, 256))
        peer = _peer_main(h2, u_b, v_b, a1, cq, a2, l, _tile(t, 512), 512)
        return x, peer, new_a, new_m, new_h

    xp, xs = x_prompt.reshape(bp * seq, d), x_sample.reshape(bs, d)
    pp = ps = None
    outs = [[] for _ in range(5)]
    for l in range(depth):
        xp, pp, pa, pm, ph = layer_fn(grp_p, l, xp, pp, None)
        xs, ps, sa, sm, st_h = layer_fn(grp_s, l, xs, ps, st_h)
        for lst, val in zip(outs, (pa, pm, ph.reshape(bp, n_heads, hd, ns), sa, sm)):
            lst.append(val)
    y_p = _norm(grp_p, depth - 1, xp, final, 0, add=pp, gate_k=5, gate_layer=depth - 1)
    y_s = _norm(grp_s, depth - 1, xs, final, 0, add=ps, gate_k=5, gate_layer=depth - 1)
    stacked = [jnp.stack(o) for o in outs]
    return (y_p.reshape(bp, seq, d), y_s.reshape(bs, 1, d), *stacked, st_h.reshape(state_ssm.shape))
```
